```python
import math
import jax, jax.numpy as jnp
from jax import lax
import numpy as np

D_MODEL = 2048
BATCH = 1
SEQ = 8192
DEPTH = 1
DEC_BATCH = 32
DEC_SEQ = 4
PAST_LEN = 16384
PAGE_SIZE = 128

DIFF_HD = 128
DIFF_HEADS = D_MODEL // 2 // (2 * DIFF_HD)
DIFF_MAPS = 2 * DIFF_HEADS
DIFF_W = DIFF_MAPS * DIFF_HD
Q_BLOCK = 128
SUBLN_EPS = 1e-5
RWKV_HD = 64
RWKV_W = D_MODEL // 2
RWKV_HEADS = RWKV_W // RWKV_HD
DECAY_LORA = 96
ICLR_LORA = 96
GATE_LORA = 256
RWKV_COLS = 3 * RWKV_W + DECAY_LORA + ICLR_LORA + GATE_LORA
IN_COLS = 3 * DIFF_W + RWKV_COLS
MIX_W = DIFF_W + RWKV_W
GN_EPS = 64e-5
MEM_LEN = 256
MEM_HEADS = 4
MEM_HD = D_MODEL // MEM_HEADS
PEER_HEADS = 8
N_KEYS = 128
N_EXPERTS = N_KEYS * N_KEYS
PEER_TOPK = 16
PEER_DK = 256
PEER_BLOCK = 128
NORM_EPS = 1e-6

kernel_name = 'diffattn_rwkv7_peer_hybrid_step'


def rmsnorm(x, g, eps=NORM_EPS):
    xf = x.astype(jnp.float32)
    y = xf * lax.rsqrt(jnp.mean(xf * xf, axis=-1, keepdims=True) + eps)
    return (y * g.astype(jnp.float32)).astype(x.dtype)


def alibi_slopes(n):
    return 2.0 ** (-8.0 * jnp.arange(1, n + 1, dtype=jnp.float32) / n)


def diff_attn_core(q, k, v, q_pos, k_pos, lam, lam_init, subln_g):
    B, Tq = q.shape[:2]
    Tk = k.shape[1]
    s = jnp.einsum('bqmd,bkmd->bmqk', q, k).astype(jnp.float32) * (DIFF_HD ** -0.5)
    dist = (q_pos[:, None] - k_pos[None, :]).astype(jnp.float32)
    s = s - alibi_slopes(DIFF_MAPS)[:, None, None] * dist
    s = jnp.where(dist >= 0, s, -jnp.inf)
    p = jax.nn.softmax(s, axis=-1).reshape(B, DIFF_HEADS, 2, Tq, Tk)
    a = p[:, :, 0] - lam * p[:, :, 1]
    o = jnp.einsum('bhqk,bkhe->bqhe', a, v.astype(jnp.float32))
    o = o * lax.rsqrt(jnp.mean(o * o, axis=-1, keepdims=True) + SUBLN_EPS) * subln_g.astype(jnp.float32)
    return (o * (1.0 - lam_init)).astype(q.dtype)


def diff_attn_prompt(q, k, v, lam, lam_init, subln_g):
    B, T = q.shape[:2]
    nb = T // Q_BLOCK
    qb = jnp.swapaxes(q.reshape(B, nb, Q_BLOCK, DIFF_MAPS, DIFF_HD), 0, 1)
    pos = jnp.arange(T, dtype=jnp.int32)

    def one(args):
        q_blk, q_pos = args
        return diff_attn_core(q_blk, k, v, q_pos, pos, lam, lam_init, subln_g)

    o = lax.map(one, (qb, pos.reshape(nb, Q_BLOCK)))
    return jnp.swapaxes(o, 0, 1).reshape(B, T, DIFF_HEADS, 2 * DIFF_HD)


def diff_attn_sample(q, k_new, v_new, cache_k, cache_v, page_table, lam, lam_init, subln_g):
    Ts = q.shape[1]
    past = page_table.shape[1] * cache_k.shape[1]
    q_pos = past + jnp.arange(Ts, dtype=jnp.int32)
    k_pos = jnp.arange(past + Ts, dtype=jnp.int32)

    def one(args):
        qb, kb, vb, pt = args
        kf = jnp.concatenate([cache_k[pt].reshape(past, DIFF_MAPS, DIFF_HD).astype(kb.dtype), kb], axis=0)
        vf = jnp.concatenate([cache_v[pt].reshape(past, DIFF_HEADS, 2 * DIFF_HD).astype(vb.dtype), vb], axis=0)
        return diff_attn_core(qb[None], kf[None], vf[None], q_pos, k_pos, lam, lam_init, subln_g)[0]

    return lax.map(one, (q, k_new, v_new, page_table))


def rwkv7_mix(p, shift_prev, wkv0, lp):
    f32 = jnp.float32
    B, T, _ = p.shape
    p_prev = jnp.concatenate([shift_prev[:, None, :].astype(p.dtype), p[:, :-1]], axis=1)
    xs = p + (p_prev - p) * lp['shift_mu']
    o1, o2, o3 = RWKV_W, 2 * RWKV_W, 3 * RWKV_W
    o4 = o3 + DECAY_LORA
    o5 = o4 + ICLR_LORA
    r, k, v = xs[..., :o1], xs[..., o1:o2], xs[..., o2:o3]
    xw, xa, xg = xs[..., o3:o4], xs[..., o4:o5], xs[..., o5:]
    w_log = -jax.nn.softplus(-(lp['w0'] + jnp.tanh(xw) @ lp['w_w2']).astype(f32)) - 0.5
    decay = jnp.exp(-jnp.exp(w_log))
    a = jax.nn.sigmoid((lp['a0'] + xa @ lp['w_a2']).astype(f32))
    g = (jax.nn.sigmoid(xg) @ lp['w_g2']).astype(f32)
    hs = (B, T, RWKV_HEADS, RWKV_HD)
    r = r.astype(f32).reshape(hs)
    v = v.astype(f32).reshape(hs)
    k = k.astype(f32)
    kk = (k * lp['k_k']).reshape(hs)
    kk = kk / jnp.maximum(jnp.sqrt(jnp.sum(kk * kk, axis=-1, keepdims=True)), 1e-12)
    k = (k * (1.0 + (a - 1.0) * lp['k_a'])).reshape(hs)
    decay = decay.reshape(hs)
    a = a.reshape(hs)

    def step(S, inp):
        r_t, w_t, k_t, v_t, kk_t, a_t = inp
        sa = jnp.einsum('bhij,bhj->bhi', S, -kk_t)
        S = S * w_t[:, :, None, :] + sa[..., None] * (kk_t * a_t)[:, :, None, :] + v_t[..., None] * k_t[:, :, None, :]
        return S, jnp.einsum('bhij,bhj->bhi', S, r_t)

    seq = (jnp.moveaxis(r, 1, 0), jnp.moveaxis(decay, 1, 0), jnp.moveaxis(k, 1, 0),
           jnp.moveaxis(v, 1, 0), jnp.moveaxis(kk, 1, 0), jnp.moveaxis(a, 1, 0))
    S_fin, y = lax.scan(step, wkv0.astype(f32), seq)
    y = jnp.moveaxis(y, 0, 1)
    mu = jnp.mean(y, axis=-1, keepdims=True)
    var = jnp.mean((y - mu) ** 2, axis=-1, keepdims=True)
    yn = ((y - mu) * lax.rsqrt(var + GN_EPS)).reshape(B, T, RWKV_W) * lp['lnx_w'].astype(f32) + lp['lnx_b'].astype(f32)
    bonus = (jnp.sum(r * k * lp['r_k'].astype(f32), axis=-1, keepdims=True) * v).reshape(B, T, RWKV_W)
    out = ((yn + bonus) * g).astype(p.dtype)
    return out, p[:, -1], S_fin


def mem_kv(mem, g_mem, w_mk, w_mv):
    B, M, _ = mem.shape
    mn = rmsnorm(mem, g_mem)
    return ((mn @ w_mk).reshape(B, M, MEM_HEADS, MEM_HD), (mn @ w_mv).reshape(B, M, MEM_HEADS, MEM_HD))


def cross_attn(hn, mk, mv, w_cq, w_co):
    B, T, _ = hn.shape
    q = (hn @ w_cq).reshape(B, T, MEM_HEADS, MEM_HD)
    s = jnp.einsum('bthd,bmhd->bhtm', q, mk.astype(q.dtype)).astype(jnp.float32) * (MEM_HD ** -0.5)
    p = jax.nn.softmax(s, axis=-1)
    o = jnp.einsum('bhtm,bmhd->bthd', p, mv.astype(jnp.float32)).astype(hn.dtype)
    return o.reshape(B, T, MEM_HEADS * MEM_HD) @ w_co


def peer_ffn(xn, w_pq, sub_keys, u, v):
    B, T, D = xn.shape
    n = B * T
    nb = -(-n // PEER_BLOCK)
    xf = jnp.pad(xn.reshape(n, D), ((0, nb * PEER_BLOCK - n), (0, 0)))

    def block(xb):
        q = (xb @ w_pq).reshape(PEER_BLOCK, PEER_HEADS, 2, PEER_DK // 2)
        s = jnp.einsum('thcd,hcnd->thcn', q, sub_keys).astype(jnp.float32)
        s1, i1 = lax.top_k(s[:, :, 0], PEER_TOPK)
        s2, i2 = lax.top_k(s[:, :, 1], PEER_TOPK)
        cand = (s1[..., :, None] + s2[..., None, :]).reshape(PEER_BLOCK, PEER_HEADS, PEER_TOPK * PEER_TOPK)
        cidx = (i1[..., :, None] * N_KEYS + i2[..., None, :]).reshape(PEER_BLOCK, PEER_HEADS, PEER_TOPK * PEER_TOPK)
        sc, pos = lax.top_k(cand, PEER_TOPK)
        idx = jnp.take_along_axis(cidx, pos, axis=-1)
        gate = jax.nn.softmax(sc, axis=-1)
        hid = jax.nn.gelu(jnp.einsum('thkd,td->thk', u[idx], xb).astype(jnp.float32), approximate=False)
        return jnp.einsum('thk,thkd->td', (gate * hid).astype(xb.dtype), v[idx])

    y = lax.map(block, xf.reshape(nb, PEER_BLOCK, D))
    return y.reshape(nb * PEER_BLOCK, D)[:n].reshape(B, T, D)


def layer_step(x, mk, mv, shift0, wkv0, attend, lp, l):
    B, T, _ = x.shape
    proj = rmsnorm(x, lp['norm1_g']) @ lp['w_in']
    q = proj[..., :DIFF_W].reshape(B, T, DIFF_MAPS, DIFF_HD)
    k = proj[..., DIFF_W:2 * DIFF_W].reshape(B, T, DIFF_MAPS, DIFF_HD)
    v = proj[..., 2 * DIFF_W:3 * DIFF_W].reshape(B, T, DIFF_HEADS, 2 * DIFF_HD)
    f32 = jnp.float32
    lam_init = 0.8 - 0.6 * math.exp(-0.3 * l)
    lam = (jnp.exp(jnp.sum(lp['lam_q1'].astype(f32) * lp['lam_k1'].astype(f32)))
           - jnp.exp(jnp.sum(lp['lam_q2'].astype(f32) * lp['lam_k2'].astype(f32))) + lam_init)
    a_out = attend(q, k, v, lam, lam_init, lp['subln_g'])
    r_out, shift_new, wkv_new = rwkv7_mix(proj[..., 3 * DIFF_W:], shift0, wkv0, lp)
    h = x + jnp.concatenate([a_out.reshape(B, T, DIFF_W), r_out], axis=-1) @ lp['w_out']
    h = h + cross_attn(rmsnorm(h, lp['norm2_g']), mk, mv, lp['w_cq'], lp['w_co'])
    h = h + peer_ffn(rmsnorm(h, lp['norm3_g']), lp['w_pq'], lp['peer_keys'], lp['peer_u'], lp['peer_v'])
    return h, k, v, shift_new, wkv_new


def setup_inputs(seed: int = 0) -> dict:
    key = jax.random.key(seed)
    ks = iter(jax.random.split(key, 64))
    L, D = DEPTH, D_MODEL

    def nrm(shape, scale):
        return jax.random.normal(next(ks), shape, jnp.float32) * scale

    def gain(shape):
        return 1.0 + nrm(shape, 0.02)

    n_pages = PAST_LEN // PAGE_SIZE
    n_used = DEC_BATCH * n_pages
    n_phys = n_used + max(1, n_used // 4)
    x_prompt = nrm((BATCH, SEQ, D), 1.0)
    x_sample = nrm((DEC_BATCH, DEC_SEQ, D), 1.0)
    cache_k = nrm((L, n_phys, PAGE_SIZE, DIFF_MAPS, DIFF_HD), 1.0)
    cache_v = nrm((L, n_phys, PAGE_SIZE, DIFF_HEADS, 2 * DIFF_HD), 1.0)
    state_wkv = nrm((L, DEC_BATCH, RWKV_HEADS, RWKV_HD, RWKV_HD), 0.3)
    state_shift = nrm((L, DEC_BATCH, RWKV_COLS), 1.0)
    cache_mem_k = nrm((L, DEC_BATCH, MEM_LEN, MEM_HEADS, MEM_HD), 1.0)
    cache_mem_v = nrm((L, DEC_BATCH, MEM_LEN, MEM_HEADS, MEM_HD), 1.0)
    page_table = jax.random.permutation(next(ks), n_phys)[:n_used].reshape(DEC_BATCH, n_pages).astype(jnp.int32)
    mem_prompt = nrm((BATCH, MEM_LEN, D), 1.0)
    return {
        'x_prompt': x_prompt, 'x_sample': x_sample,
        'cache_k': cache_k, 'cache_v': cache_v,
        'state_wkv': state_wkv, 'state_shift': state_shift,
        'cache_mem_k': cache_mem_k, 'cache_mem_v': cache_mem_v,
        'page_table': page_table, 'mem_prompt': mem_prompt,
        'norm1_g': gain((L, D)),
        'w_in': nrm((L, D, IN_COLS), D ** -0.5),
        'lam_q1': nrm((L, DIFF_HD), 0.1), 'lam_k1': nrm((L, DIFF_HD), 0.1),
        'lam_q2': nrm((L, DIFF_HD), 0.1), 'lam_k2': nrm((L, DIFF_HD), 0.1),
        'subln_g': gain((L, 2 * DIFF_HD)),
        'shift_mu': jax.random.uniform(next(ks), (L, RWKV_COLS), jnp.float32),
        'w0': nrm((L, RWKV_W), 1.0) - 2.0,
        'w_w2': nrm((L, DECAY_LORA, RWKV_W), DECAY_LORA ** -0.5),
        'a0': nrm((L, RWKV_W), 0.1),
        'w_a2': nrm((L, ICLR_LORA, RWKV_W), 0.5 * ICLR_LORA ** -0.5),
        'w_g2': nrm((L, GATE_LORA, RWKV_W), GATE_LORA ** -0.5),
        'k_k': 0.85 + nrm((L, RWKV_W), 0.02),
        'k_a': gain((L, RWKV_W)),
        'r_k': nrm((L, RWKV_HEADS, RWKV_HD), 0.1),
        'lnx_w': gain((L, RWKV_W)),
        'lnx_b': nrm((L, RWKV_W), 0.02),
        'w_out': nrm((L, MIX_W, D), MIX_W ** -0.5),
        'norm2_g': gain((L, D)),
        'mem_norm_g': gain((L, D)),
        'w_cq': nrm((L, D, MEM_HEADS * MEM_HD), D ** -0.5),
        'w_mk': nrm((L, D, MEM_HEADS * MEM_HD), D ** -0.5),
        'w_mv': nrm((L, D, MEM_HEADS * MEM_HD), D ** -0.5),
        'w_co': nrm((L, MEM_HEADS * MEM_HD, D), (MEM_HEADS * MEM_HD) ** -0.5),
        'norm3_g': gain((L, D)),
        'w_pq': nrm((L, D, PEER_HEADS * PEER_DK), D ** -0.5),
        'peer_keys': nrm((L, PEER_HEADS, 2, N_KEYS, PEER_DK // 2), (PEER_DK // 2) ** -0.5),
        'peer_u': nrm((L, N_EXPERTS, D), D ** -0.5),
        'peer_v': nrm((L, N_EXPERTS, D), 0.3),
        'final_g': gain((D,)),
    }


def reference(x_prompt, x_sample, cache_k, cache_v, state_wkv, state_shift, cache_mem_k, cache_mem_v,
              page_table, mem_prompt, norm1_g, w_in, lam_q1, lam_k1, lam_q2, lam_k2, subln_g, shift_mu,
              w0, w_w2, a0, w_a2, w_g2, k_k, k_a, r_k, lnx_w, lnx_b, w_out, norm2_g, mem_norm_g,
              w_cq, w_mk, w_mv, w_co, norm3_g, w_pq, peer_keys, peer_u, peer_v, final_g):
    xp, xs = x_prompt, x_sample
    Bp = xp.shape[0]
    kp_l, vp_l, ks_l, vs_l = [], [], [], []
    wkvp_l, shp_l, wkvs_l, shs_l, mkp_l, mvp_l = [], [], [], [], [], []
    for l in range(DEPTH):
        lp = dict(norm1_g=norm1_g[l], w_in=w_in[l], lam_q1=lam_q1[l], lam_k1=lam_k1[l],
                  lam_q2=lam_q2[l], lam_k2=lam_k2[l], subln_g=subln_g[l], shift_mu=shift_mu[l],
                  w0=w0[l], w_w2=w_w2[l], a0=a0[l], w_a2=w_a2[l], w_g2=w_g2[l], k_k=k_k[l], k_a=k_a[l],
                  r_k=r_k[l], lnx_w=lnx_w[l], lnx_b=lnx_b[l], w_out=w_out[l], norm2_g=norm2_g[l],
                  w_cq=w_cq[l], w_co=w_co[l], norm3_g=norm3_g[l], w_pq=w_pq[l], peer_keys=peer_keys[l],
                  peer_u=peer_u[l], peer_v=peer_v[l])
        mk_p, mv_p = mem_kv(mem_prompt, mem_norm_g[l], w_mk[l], w_mv[l])
        shift0 = jnp.zeros((Bp, RWKV_COLS), xp.dtype)
        wkv0 = jnp.zeros((Bp, RWKV_HEADS, RWKV_HD, RWKV_HD), jnp.float32)
        xp, k_p, v_p, sh_p, wkv_p = layer_step(xp, mk_p, mv_p, shift0, wkv0, diff_attn_prompt, lp, l)
        ck, cv = cache_k[l], cache_v[l]
        attend_s = lambda q, k, v, lam, li, g, ck=ck, cv=cv: diff_attn_sample(q, k, v, ck, cv, page_table, lam, li, g)
        xs, k_s, v_s, sh_s, wkv_s = layer_step(xs, cache_mem_k[l], cache_mem_v[l], state_shift[l], state_wkv[l],
                                               attend_s, lp, l)
        kp_l.append(k_p); vp_l.append(v_p); ks_l.append(k_s); vs_l.append(v_s)
        wkvp_l.append(wkv_p.astype(xp.dtype)); shp_l.append(sh_p)
        wkvs_l.append(wkv_s.astype(state_wkv.dtype)); shs_l.append(sh_s.astype(state_shift.dtype))
        mkp_l.append(mk_p); mvp_l.append(mv_p)
    y_prompt = rmsnorm(xp, final_g)
    y_sample = rmsnorm(xs, final_g)
    return (y_prompt, y_sample, jnp.stack(kp_l), jnp.stack(vp_l), jnp.stack(ks_l), jnp.stack(vs_l),
            jnp.stack(wkvp_l), jnp.stack(shp_l), jnp.stack(wkvs_l), jnp.stack(shs_l),
            jnp.stack(mkp_l), jnp.stack(mvp_l))
```

```python
import functools
import math

import jax
import jax.numpy as jnp
from jax import lax
from jax.experimental import pallas as pl
from jax.experimental.pallas import tpu as pltpu

D_MODEL = 2048
DIFF_HD = 128
DIFF_HEADS = 4
DIFF_MAPS = 8
DIFF_W = 1024
Q_BLOCK = 128
SUBLN_EPS = 1e-5
RWKV_HD = 64
RWKV_W = 1024
RWKV_HEADS = 16
DECAY_LORA = 96
ICLR_LORA = 96
GATE_LORA = 256
RWKV_COLS = 3 * RWKV_W + DECAY_LORA + ICLR_LORA + GATE_LORA
GN_EPS = 64e-5
MEM_HEADS = 4
MEM_HD = D_MODEL // MEM_HEADS
PEER_HEADS = 8
N_KEYS = 128
PEER_TOPK = 16
PEER_DK = 256
PEER_BLOCK = 128
NORM_EPS = 1e-6


def _rmsnorm_kernel(x_ref, g_ref, o_ref):
    x = x_ref[...]
    y = x * lax.rsqrt(jnp.mean(x * x, axis=-1, keepdims=True) + NORM_EPS)
    o_ref[...] = y * g_ref[...]


def _rmsnorm_pallas(x, g, rows=256):
    n, d = x.shape
    rows = min(rows, n)
    return pl.pallas_call(
        _rmsnorm_kernel,
        grid=(n // rows,),
        in_specs=[pl.BlockSpec((rows, d), lambda i: (i, 0)), pl.BlockSpec((1, d), lambda i: (0, 0))],
        out_specs=pl.BlockSpec((rows, d), lambda i: (i, 0)),
        out_shape=jax.ShapeDtypeStruct((n, d), x.dtype),
        name="rmsnorm",
    )(x, g.reshape(1, d))


def rmsnorm(x, g, eps=NORM_EPS):
    xf = x.astype(jnp.float32)
    y = xf * lax.rsqrt(jnp.mean(xf * xf, axis=-1, keepdims=True) + eps)
    return (y * g.astype(jnp.float32)).astype(x.dtype)


def alibi_slopes(n):
    return 2.0 ** (-8.0 * jnp.arange(1, n + 1, dtype=jnp.float32) / n)


def diff_attn_core(q, k, v, q_pos, k_pos, lam, lam_init, subln_g):
    B, Tq = q.shape[:2]
    Tk = k.shape[1]
    s = jnp.einsum('bqmd,bkmd->bmqk', q, k).astype(jnp.float32) * (DIFF_HD ** -0.5)
    dist = (q_pos[:, None] - k_pos[None, :]).astype(jnp.float32)
    s = s - alibi_slopes(DIFF_MAPS)[:, None, None] * dist
    s = jnp.where(dist >= 0, s, -jnp.inf)
    p = jax.nn.softmax(s, axis=-1).reshape(B, DIFF_HEADS, 2, Tq, Tk)
    a = p[:, :, 0] - lam * p[:, :, 1]
    o = jnp.einsum('bhqk,bkhe->bqhe', a, v.astype(jnp.float32))
    o = o * lax.rsqrt(jnp.mean(o * o, axis=-1, keepdims=True) + SUBLN_EPS) * subln_g.astype(jnp.float32)
    return (o * (1.0 - lam_init)).astype(q.dtype)


def diff_attn_prompt(q, k, v, lam, lam_init, subln_g):
    B, T = q.shape[:2]
    nb = T // Q_BLOCK
    qb = jnp.swapaxes(q.reshape(B, nb, Q_BLOCK, DIFF_MAPS, DIFF_HD), 0, 1)
    pos = jnp.arange(T, dtype=jnp.int32)

    def one(args):
        q_blk, q_pos = args
        return diff_attn_core(q_blk, k, v, q_pos, pos, lam, lam_init, subln_g)

    o = lax.map(one, (qb, pos.reshape(nb, Q_BLOCK)))
    return jnp.swapaxes(o, 0, 1).reshape(B, T, DIFF_HEADS, 2 * DIFF_HD)


def diff_attn_sample(q, k_new, v_new, cache_k, cache_v, page_table, lam, lam_init, subln_g):
    Ts = q.shape[1]
    past = page_table.shape[1] * cache_k.shape[1]
    q_pos = past + jnp.arange(Ts, dtype=jnp.int32)
    k_pos = jnp.arange(past + Ts, dtype=jnp.int32)

    def one(args):
        qb, kb, vb, pt = args
        kf = jnp.concatenate([cache_k[pt].reshape(past, DIFF_MAPS, DIFF_HD).astype(kb.dtype), kb], axis=0)
        vf = jnp.concatenate([cache_v[pt].reshape(past, DIFF_HEADS, 2 * DIFF_HD).astype(vb.dtype), vb], axis=0)
        return diff_attn_core(qb[None], kf[None], vf[None], q_pos, k_pos, lam, lam_init, subln_g)[0]

    return lax.map(one, (q, k_new, v_new, page_table))


def rwkv7_mix(p, shift_prev, wkv0, lp):
    f32 = jnp.float32
    B, T, _ = p.shape
    p_prev = jnp.concatenate([shift_prev[:, None, :].astype(p.dtype), p[:, :-1]], axis=1)
    xs = p + (p_prev - p) * lp['shift_mu']
    o1, o2, o3 = RWKV_W, 2 * RWKV_W, 3 * RWKV_W
    o4 = o3 + DECAY_LORA
    o5 = o4 + ICLR_LORA
    r, k, v = xs[..., :o1], xs[..., o1:o2], xs[..., o2:o3]
    xw, xa, xg = xs[..., o3:o4], xs[..., o4:o5], xs[..., o5:]
    w_log = -jax.nn.softplus(-(lp['w0'] + jnp.tanh(xw) @ lp['w_w2']).astype(f32)) - 0.5
    decay = jnp.exp(-jnp.exp(w_log))
    a = jax.nn.sigmoid((lp['a0'] + xa @ lp['w_a2']).astype(f32))
    g = (jax.nn.sigmoid(xg) @ lp['w_g2']).astype(f32)
    hs = (B, T, RWKV_HEADS, RWKV_HD)
    r = r.astype(f32).reshape(hs)
    v = v.astype(f32).reshape(hs)
    k = k.astype(f32)
    kk = (k * lp['k_k']).reshape(hs)
    kk = kk / jnp.maximum(jnp.sqrt(jnp.sum(kk * kk, axis=-1, keepdims=True)), 1e-12)
    k = (k * (1.0 + (a - 1.0) * lp['k_a'])).reshape(hs)
    decay = decay.reshape(hs)
    a = a.reshape(hs)

    def step(S, inp):
        r_t, w_t, k_t, v_t, kk_t, a_t = inp
        sa = jnp.einsum('bhij,bhj->bhi', S, -kk_t)
        S = S * w_t[:, :, None, :] + sa[..., None] * (kk_t * a_t)[:, :, None, :] + v_t[..., None] * k_t[:, :, None, :]
        return S, jnp.einsum('bhij,bhj->bhi', S, r_t)

    seq = (jnp.moveaxis(r, 1, 0), jnp.moveaxis(decay, 1, 0), jnp.moveaxis(k, 1, 0),
           jnp.moveaxis(v, 1, 0), jnp.moveaxis(kk, 1, 0), jnp.moveaxis(a, 1, 0))
    S_fin, y = lax.scan(step, wkv0.astype(f32), seq)
    y = jnp.moveaxis(y, 0, 1)
    mu = jnp.mean(y, axis=-1, keepdims=True)
    var = jnp.mean((y - mu) ** 2, axis=-1, keepdims=True)
    yn = ((y - mu) * lax.rsqrt(var + GN_EPS)).reshape(B, T, RWKV_W) * lp['lnx_w'].astype(f32) + lp['lnx_b'].astype(f32)
    bonus = (jnp.sum(r * k * lp['r_k'].astype(f32), axis=-1, keepdims=True) * v).reshape(B, T, RWKV_W)
    out = ((yn + bonus) * g).astype(p.dtype)
    return out, p[:, -1], S_fin


def mem_kv(mem, g_mem, w_mk, w_mv):
    B, M, _ = mem.shape
    mn = rmsnorm(mem, g_mem)
    return ((mn @ w_mk).reshape(B, M, MEM_HEADS, MEM_HD), (mn @ w_mv).reshape(B, M, MEM_HEADS, MEM_HD))


def cross_attn(hn, mk, mv, w_cq, w_co):
    B, T, _ = hn.shape
    q = (hn @ w_cq).reshape(B, T, MEM_HEADS, MEM_HD)
    s = jnp.einsum('bthd,bmhd->bhtm', q, mk.astype(q.dtype)).astype(jnp.float32) * (MEM_HD ** -0.5)
    p = jax.nn.softmax(s, axis=-1)
    o = jnp.einsum('bhtm,bmhd->bthd', p, mv.astype(jnp.float32)).astype(hn.dtype)
    return o.reshape(B, T, MEM_HEADS * MEM_HD) @ w_co


def peer_ffn(xn, w_pq, sub_keys, u, v):
    B, T, D = xn.shape
    n = B * T
    nb = -(-n // PEER_BLOCK)
    xf = jnp.pad(xn.reshape(n, D), ((0, nb * PEER_BLOCK - n), (0, 0)))

    def block(xb):
        q = (xb @ w_pq).reshape(PEER_BLOCK, PEER_HEADS, 2, PEER_DK // 2)
        s = jnp.einsum('thcd,hcnd->thcn', q, sub_keys).astype(jnp.float32)
        s1, i1 = lax.top_k(s[:, :, 0], PEER_TOPK)
        s2, i2 = lax.top_k(s[:, :, 1], PEER_TOPK)
        cand = (s1[..., :, None] + s2[..., None, :]).reshape(PEER_BLOCK, PEER_HEADS, PEER_TOPK * PEER_TOPK)
        cidx = (i1[..., :, None] * N_KEYS + i2[..., None, :]).reshape(PEER_BLOCK, PEER_HEADS, PEER_TOPK * PEER_TOPK)
        sc, pos = lax.top_k(cand, PEER_TOPK)
        idx = jnp.take_along_axis(cidx, pos, axis=-1)
        gate = jax.nn.softmax(sc, axis=-1)
        hid = jax.nn.gelu(jnp.einsum('thkd,td->thk', u[idx], xb).astype(jnp.float32), approximate=False)
        return jnp.einsum('thk,thkd->td', (gate * hid).astype(xb.dtype), v[idx])

    y = lax.map(block, xf.reshape(nb, PEER_BLOCK, D))
    return y.reshape(nb * PEER_BLOCK, D)[:n].reshape(B, T, D)


def layer_step(x, mk, mv, shift0, wkv0, attend, lp, l):
    B, T, _ = x.shape
    proj = rmsnorm(x, lp['norm1_g']) @ lp['w_in']
    q = proj[..., :DIFF_W].reshape(B, T, DIFF_MAPS, DIFF_HD)
    k = proj[..., DIFF_W:2 * DIFF_W].reshape(B, T, DIFF_MAPS, DIFF_HD)
    v = proj[..., 2 * DIFF_W:3 * DIFF_W].reshape(B, T, DIFF_HEADS, 2 * DIFF_HD)
    f32 = jnp.float32
    lam_init = 0.8 - 0.6 * math.exp(-0.3 * l)
    lam = (jnp.exp(jnp.sum(lp['lam_q1'].astype(f32) * lp['lam_k1'].astype(f32)))
           - jnp.exp(jnp.sum(lp['lam_q2'].astype(f32) * lp['lam_k2'].astype(f32))) + lam_init)
    a_out = attend(q, k, v, lam, lam_init, lp['subln_g'])
    r_out, shift_new, wkv_new = rwkv7_mix(proj[..., 3 * DIFF_W:], shift0, wkv0, lp)
    h = x + jnp.concatenate([a_out.reshape(B, T, DIFF_W), r_out], axis=-1) @ lp['w_out']
    h = h + cross_attn(rmsnorm(h, lp['norm2_g']), mk, mv, lp['w_cq'], lp['w_co'])
    h = h + peer_ffn(rmsnorm(h, lp['norm3_g']), lp['w_pq'], lp['peer_keys'], lp['peer_u'], lp['peer_v'])
    return h, k, v, shift_new, wkv_new


def kernel(x_prompt, x_sample, cache_k, cache_v, state_wkv, state_shift, cache_mem_k, cache_mem_v, page_table, mem_prompt, norm1_g, w_in, lam_q1, lam_k1, lam_q2, lam_k2, subln_g, shift_mu, w0, w_w2, a0, w_a2, w_g2, k_k, k_a, r_k, lnx_w, lnx_b, w_out, norm2_g, mem_norm_g, w_cq, w_mk, w_mv, w_co, norm3_g, w_pq, peer_keys, peer_u, peer_v, final_g):
    xp, xs = x_prompt, x_sample
    Bp = xp.shape[0]
    l = 0
    lp = dict(norm1_g=norm1_g[l], w_in=w_in[l], lam_q1=lam_q1[l], lam_k1=lam_k1[l],
              lam_q2=lam_q2[l], lam_k2=lam_k2[l], subln_g=subln_g[l], shift_mu=shift_mu[l],
              w0=w0[l], w_w2=w_w2[l], a0=a0[l], w_a2=w_a2[l], w_g2=w_g2[l], k_k=k_k[l], k_a=k_a[l],
              r_k=r_k[l], lnx_w=lnx_w[l], lnx_b=lnx_b[l], w_out=w_out[l], norm2_g=norm2_g[l],
              w_cq=w_cq[l], w_co=w_co[l], norm3_g=norm3_g[l], w_pq=w_pq[l], peer_keys=peer_keys[l],
              peer_u=peer_u[l], peer_v=peer_v[l])
    mk_p, mv_p = mem_kv(mem_prompt, mem_norm_g[l], w_mk[l], w_mv[l])
    shift0 = jnp.zeros((Bp, RWKV_COLS), xp.dtype)
    wkv0 = jnp.zeros((Bp, RWKV_HEADS, RWKV_HD, RWKV_HD), jnp.float32)
    xp, k_p, v_p, sh_p, wkv_p = layer_step(xp, mk_p, mv_p, shift0, wkv0, diff_attn_prompt, lp, l)
    ck, cv = cache_k[l], cache_v[l]
    attend_s = lambda q, k, v, lam, li, g: diff_attn_sample(q, k, v, ck, cv, page_table, lam, li, g)
    xs, k_s, v_s, sh_s, wkv_s = layer_step(xs, cache_mem_k[l], cache_mem_v[l], state_shift[l], state_wkv[l],
                                           attend_s, lp, l)
    y_prompt = _rmsnorm_pallas(xp.reshape(-1, D_MODEL), final_g).reshape(xp.shape)
    y_sample = _rmsnorm_pallas(xs.reshape(-1, D_MODEL), final_g).reshape(xs.shape)
    return (y_prompt, y_sample, k_p[None], v_p[None], k_s[None], v_s[None],
            wkv_p[None], sh_p[None], wkv_s[None], sh_s[None], mk_p[None], mv_p[None])
```

```python
import functools
import math

import jax
import jax.numpy as jnp
from jax import lax
from jax.experimental import pallas as pl
from jax.experimental.pallas import tpu as pltpu

f32, bf16 = jnp.float32, jnp.bfloat16

D_MODEL = 2048
DIFF_HD = 128
DIFF_HEADS = 4
DIFF_MAPS = 8
DIFF_W = 1024
SUBLN_EPS = 1e-5
LAM_INIT = 0.8 - 0.6 * math.exp(-0.3 * 0)
RWKV_HD = 64
RWKV_W = 1024
RWKV_HEADS = 16
DECAY_LORA = 96
ICLR_LORA = 96
GATE_LORA = 256
RWKV_COLS = 3 * RWKV_W + DECAY_LORA + ICLR_LORA + GATE_LORA
GN_EPS = 64e-5
MEM_HEADS = 4
MEM_HD = D_MODEL // MEM_HEADS
PEER_HEADS = 8
N_KEYS = 128
PEER_TOPK = 16
PEER_DK = 256
NORM_EPS = 1e-6

VMEM_LIMIT_BYTES = 60 << 20
LANES = 128

RWKV_CHUNK = 64
HEAD_PAIR_LANES = 2 * RWKV_HD
RWKV_PAIRS_PER_STEP = 4
FLASH_BLOCK = 512
PEER_TOKEN_BLOCK_MAX = 640
PEER_EXPERT_CHUNK = 1024
MATMUL_ROWS = 512


def _peer_token_block(n_tokens):
    return max(b for b in range(LANES, PEER_TOKEN_BLOCK_MAX + 1, LANES) if n_tokens % b == 0)

_NN = (((1,), (0,)), ((), ()))
_NT = (((1,), (1,)), ((), ()))


def _matmul_kernel(*refs, norm, residual):
    x_ref, refs = refs[0], refs[1:]
    if norm:
        g_ref, refs = refs[0], refs[1:]
    w_ref, refs = refs[0], refs[1:]
    if residual:
        r_ref, refs = refs[0], refs[1:]
    o_ref = refs[0]
    x = x_ref[...]
    if norm:
        x = x * lax.rsqrt(jnp.mean(x * x, axis=-1, keepdims=True) + NORM_EPS) * g_ref[...]
    y = lax.dot_general(x.astype(bf16), w_ref[...], _NN, preferred_element_type=f32)
    if residual:
        y = r_ref[...] + y
    o_ref[...] = y


def _matmul(x, w_b, gain=None, residual=None, name="matmul"):
    M, K = x.shape
    N = w_b.shape[1]
    bm = min(MATMUL_ROWS, M)
    operands = [x]
    specs = [pl.BlockSpec((bm, K), lambda i: (i, 0))]
    if gain is not None:
        operands.append(gain.reshape(1, K))
        specs.append(pl.BlockSpec((1, K), lambda i: (0, 0)))
    operands.append(w_b)
    specs.append(pl.BlockSpec((K, N), lambda i: (0, 0), pipeline_mode=pl.Buffered(1)))
    if residual is not None:
        operands.append(residual)
        specs.append(pl.BlockSpec((bm, N), lambda i: (i, 0)))
    return pl.pallas_call(
        functools.partial(_matmul_kernel, norm=gain is not None, residual=residual is not None),
        grid=(M // bm,),
        in_specs=specs,
        out_specs=pl.BlockSpec((bm, N), lambda i: (i, 0)),
        out_shape=jax.ShapeDtypeStruct((M, N), f32),
        compiler_params=pltpu.CompilerParams(dimension_semantics=("parallel",), vmem_limit_bytes=VMEM_LIMIT_BYTES),
        name=name,
    )(*operands)


def _diff_lambda(lam_ref):
    lq1, lk1, lq2, lk2 = (lam_ref[i:i + 1, :] for i in range(4))
    return (jnp.exp(jnp.sum(lq1 * lk1, axis=-1, keepdims=True))
            - jnp.exp(jnp.sum(lq2 * lk2, axis=-1, keepdims=True)) + LAM_INIT)


def _diff_finish(o0, o1, lam, g):
    o = o0 - lam * o1
    o = o * lax.rsqrt(jnp.mean(o * o, axis=-1, keepdims=True) + SUBLN_EPS) * g
    return o * (1.0 - LAM_INIT)


def _flash_prompt_kernel(q_ref, k_ref, v_ref, lam_ref, g_ref, o_ref, m_scr, l_scr, acc, bias_scr, *, blk):
    h = pl.program_id(0)
    i = pl.program_id(1)
    j = pl.program_id(2)

    @pl.when(j == 0)
    def _():
        m_scr[...] = jnp.full(m_scr.shape, -jnp.inf, f32)
        l_scr[...] = jnp.zeros(l_scr.shape, f32)
        acc[...] = jnp.zeros(acc.shape, f32)
        back = (lax.broadcasted_iota(jnp.int32, (blk, blk), 0)
                - lax.broadcasted_iota(jnp.int32, (blk, blk), 1)).astype(f32)
        for m in range(2):
            slope = jnp.exp2(-(2 * h + m + 1).astype(f32))
            bias_scr[m, 0] = -slope * back
            bias_scr[m, 1] = jnp.where(back >= 0, -slope * back, -jnp.inf)

    @pl.when(j <= i)
    def _():
        diag = (j == i).astype(jnp.int32)
        v = v_ref[...].astype(bf16)
        for m in range(2):
            slope = jnp.exp2(-(2 * h + m + 1).astype(f32))
            shift = -slope * ((i - j) * blk).astype(f32)
            q = (q_ref[:, m * DIFF_HD:(m + 1) * DIFF_HD] * (DIFF_HD ** -0.5)).astype(bf16)
            k = k_ref[:, m * DIFF_HD:(m + 1) * DIFF_HD].astype(bf16)
            s = lax.dot_general(q, k, _NT, preferred_element_type=f32) + bias_scr[m, diag]
            m_old = m_scr[m]
            m_new = jnp.maximum(m_old, jnp.max(s, axis=-1, keepdims=True) + shift)
            p = jnp.exp(s - (m_new[:, :1] - shift))
            alpha = jnp.exp(m_old - m_new)
            l_scr[m] = alpha * l_scr[m] + jnp.sum(p, axis=-1, keepdims=True)
            acc[m] = alpha[:, :1] * acc[m] + lax.dot_general(p.astype(bf16), v, _NN, preferred_element_type=f32)
            m_scr[m] = m_new

    @pl.when(j == i)
    def _():
        o0 = acc[0] / l_scr[0][:, :1]
        o1 = acc[1] / l_scr[1][:, :1]
        o_ref[...] = _diff_finish(o0, o1, _diff_lambda(lam_ref), g_ref[...])


def _diff_attn_prompt(qkv, lam4, subln_g):
    T = qkv.shape[0]
    blk = FLASH_BLOCK
    nb = T // blk
    W = 2 * DIFF_HD
    kv_idx = lambda off: (lambda h, i, j: (jnp.minimum(j, i), off + h))
    return pl.pallas_call(
        functools.partial(_flash_prompt_kernel, blk=blk),
        grid=(DIFF_HEADS, nb, nb),
        in_specs=[pl.BlockSpec((blk, W), lambda h, i, j: (i, h)),
                  pl.BlockSpec((blk, W), kv_idx(DIFF_HEADS)),
                  pl.BlockSpec((blk, W), kv_idx(2 * DIFF_HEADS)),
                  pl.BlockSpec((4, DIFF_HD), lambda h, i, j: (0, 0)),
                  pl.BlockSpec((1, W), lambda h, i, j: (0, 0))],
        out_specs=pl.BlockSpec((blk, W), lambda h, i, j: (i, h)),
        out_shape=jax.ShapeDtypeStruct((T, DIFF_HEADS * W), f32),
        scratch_shapes=[pltpu.VMEM((2, blk, LANES), f32), pltpu.VMEM((2, blk, LANES), f32),
                        pltpu.VMEM((2, blk, W), f32), pltpu.VMEM((2, 2, blk, blk), f32)],
        compiler_params=pltpu.CompilerParams(dimension_semantics=("parallel", "parallel", "arbitrary"),
                                             vmem_limit_bytes=VMEM_LIMIT_BYTES),
        name="diff_attn_prompt",
    )(qkv, qkv, qkv, lam4, subln_g.reshape(1, W))


def _alibi_slopes(n):
    return 2.0 ** (-8.0 * jnp.arange(1, n + 1, dtype=f32) / n)


def _diff_attn_core(q, k, v, q_pos, k_pos, lam, subln_g):
    B, Tq = q.shape[:2]
    Tk = k.shape[1]
    s = jnp.einsum('bqmd,bkmd->bmqk', q, k).astype(f32) * (DIFF_HD ** -0.5)
    dist = (q_pos[:, None] - k_pos[None, :]).astype(f32)
    s = s - _alibi_slopes(DIFF_MAPS)[:, None, None] * dist
    s = jnp.where(dist >= 0, s, -jnp.inf)
    p = jax.nn.softmax(s, axis=-1).reshape(B, DIFF_HEADS, 2, Tq, Tk)
    a = p[:, :, 0] - lam * p[:, :, 1]
    o = jnp.einsum('bhqk,bkhe->bqhe', a, v.astype(f32))
    o = o * lax.rsqrt(jnp.mean(o * o, axis=-1, keepdims=True) + SUBLN_EPS) * subln_g.astype(f32)
    return (o * (1.0 - LAM_INIT)).astype(q.dtype)


def _diff_attn_sample(q, k_new, v_new, cache_k, cache_v, page_table, lam, subln_g):
    Ts = q.shape[1]
    past = page_table.shape[1] * cache_k.shape[1]
    q_pos = past + jnp.arange(Ts, dtype=jnp.int32)
    k_pos = jnp.arange(past + Ts, dtype=jnp.int32)

    def one(args):
        qb, kb, vb, pt = args
        kf = jnp.concatenate([cache_k[pt].reshape(past, DIFF_MAPS, DIFF_HD), kb], axis=0)
        vf = jnp.concatenate([cache_v[pt].reshape(past, DIFF_HEADS, 2 * DIFF_HD), vb], axis=0)
        return _diff_attn_core(qb[None], kf[None], vf[None], q_pos, k_pos, lam, subln_g)[0]

    return lax.map(one, (q, k_new, v_new, page_table))


def _split(x):
    hi = x.astype(bf16)
    lo = (x - hi.astype(f32)).astype(bf16)
    return hi, lo


def _dot3(a, b, dims=_NN):
    dg = lambda x, y: lax.dot_general(x, y, dims, preferred_element_type=f32)
    return dg(a[0], b[0]) + dg(a[0], b[1]) + dg(a[1], b[0])


def _rwkv_scan_kernel(r_ref, lw_ref, k_ref, v_ref, kk_ref, a_ref, s0_ref, y_ref, sT_ref, state, *, pairs):
    c = pl.program_id(2)

    @pl.when(c == 0)
    def _():
        state[...] = s0_ref[...]

    C = r_ref.shape[0]
    C2 = 2 * C
    row = lax.broadcasted_iota(jnp.int32, (C2, C2), 0)
    col = lax.broadcasted_iota(jnp.int32, (C2, C2), 1)
    tril_strict = col < row
    tril_incl = col <= row
    eye = row == col
    first_head = lax.broadcasted_iota(jnp.int32, (C, HEAD_PAIR_LANES), 1) < RWKV_HD
    cum = (lax.broadcasted_iota(jnp.int32, (C, C), 1) <= lax.broadcasted_iota(jnp.int32, (C, C), 0)).astype(bf16)

    def stack(x):
        return jnp.concatenate([jnp.where(first_head, x, 0.0), jnp.where(first_head, 0.0, x)], axis=0)

    for p in range(pairs):
        sl = slice(p * HEAD_PAIR_LANES, (p + 1) * HEAD_PAIR_LANES)
        lw = lw_ref[:, sl]
        h1 = lw.astype(bf16)
        r1 = lw - h1.astype(f32)
        h2 = r1.astype(bf16)
        h3 = (r1 - h2.astype(f32)).astype(bf16)
        dn = lambda x: lax.dot_general(cum, x, _NN, preferred_element_type=f32)
        cs = dn(h1) + dn(h2) + dn(h3)
        cl = cs[C - 1:C, :]
        kk = kk_ref[:, sl]
        kv = k_ref[:, sl]
        b = kk * a_ref[:, sl]
        e_neg = jnp.exp(-cs)
        e_end = jnp.exp(cl - cs)
        A_s = stack(-kk * jnp.exp(cs - lw))
        R_s = stack(r_ref[:, sl] * jnp.exp(cs))
        sB = _split(stack(b * e_neg))
        sK = _split(stack(kv * e_neg))
        sV = _split(stack(v_ref[:, sl]))
        sA = _split(A_s)
        sR = _split(R_s)
        BhT = stack(b * e_end).T
        KhT = stack(kv * e_end).T
        L = jnp.where(tril_strict, _dot3(sA, sB, _NT), 0.0)
        Mak = jnp.where(tril_strict, _dot3(sA, sK, _NT), 0.0)
        Mrb = jnp.where(tril_incl, _dot3(sR, sB, _NT), 0.0)
        Mrk = jnp.where(tril_incl, _dot3(sR, sK, _NT), 0.0)
        T = jnp.where(eye, 1.0, 0.0).astype(f32) + L
        Lp = L
        n = 2
        while n < C:
            sLp = _split(Lp)
            Lp = _dot3(sLp, sLp)
            T = T + _dot3(_split(T), _split(Lp))
            n *= 2
        X1 = _dot3(_split(Mak), sV)
        sPU = _split(_dot3(_split(T), _split(jnp.concatenate([A_s, X1], axis=1))))
        QY = _dot3(_split(Mrb), sPU)
        Q = R_s + QY[:, :HEAD_PAIR_LANES]
        Yl = QY[:, HEAD_PAIR_LANES:] + _dot3(_split(Mrk), sV)
        GH = _dot3(_split(BhT), sPU)
        G = jnp.where(eye, jnp.exp(cl), 0.0) + GH[:, :HEAD_PAIR_LANES]
        H = GH[:, HEAD_PAIR_LANES:] + _dot3(_split(KhT), sV)
        sS = _split(state[p])
        Y = _dot3(_split(Q), sS) + Yl
        state[p] = _dot3(_split(G), sS) + H
        y_ref[:, sl] = Y[:C] + Y[C:]

    @pl.when(c == pl.num_programs(2) - 1)
    def _():
        sT_ref[...] = state[...]


def _rwkv_scan(r, lw, k, v, kk, a, s0):
    B, T, W = r.shape
    C = RWKV_CHUNK
    pairs = RWKV_PAIRS_PER_STEP
    groups = W // HEAD_PAIR_LANES // pairs
    seq = pl.BlockSpec((None, C, pairs * HEAD_PAIR_LANES), lambda b_, g, c: (b_, c, g))
    st = pl.BlockSpec((None, pairs, HEAD_PAIR_LANES, HEAD_PAIR_LANES), lambda b_, g, c: (b_, g, 0, 0))
    return pl.pallas_call(
        functools.partial(_rwkv_scan_kernel, pairs=pairs),
        grid=(B, groups, T // C),
        in_specs=[seq] * 6 + [st],
        out_specs=[seq, st],
        out_shape=[jax.ShapeDtypeStruct((B, T, W), f32), jax.ShapeDtypeStruct(s0.shape, f32)],
        scratch_shapes=[pltpu.VMEM((pairs, HEAD_PAIR_LANES, HEAD_PAIR_LANES), f32)],
        compiler_params=pltpu.CompilerParams(dimension_semantics=("parallel", "parallel", "arbitrary")),
        name="rwkv_scan",
    )(r, lw, k, v, kk, a, s0)


def _state_to_stacked(S):
    B, H, N, _ = S.shape
    St = jnp.swapaxes(S, -1, -2).reshape(B, H // 2, 2, N, N)
    z = jnp.zeros_like(St[:, :, 0])
    top = jnp.concatenate([St[:, :, 0], z], axis=-1)
    bot = jnp.concatenate([z, St[:, :, 1]], axis=-1)
    return jnp.concatenate([top, bot], axis=-2)


def _stacked_to_state(Ss):
    B, P, _, _ = Ss.shape
    N = RWKV_HD
    St = jnp.stack([Ss[:, :, :N, :N], Ss[:, :, N:, N:]], axis=2).reshape(B, 2 * P, N, N)
    return jnp.swapaxes(St, -1, -2)


def _rwkv7_mix(p, shift_prev, wkv0, lp):
    B, T, _ = p.shape
    p_prev = jnp.concatenate([shift_prev[:, None, :], p[:, :-1]], axis=1)
    xs = p + (p_prev - p) * lp['shift_mu']
    o1, o2, o3 = RWKV_W, 2 * RWKV_W, 3 * RWKV_W
    o4 = o3 + DECAY_LORA
    o5 = o4 + ICLR_LORA
    r, k, v = xs[..., :o1], xs[..., o1:o2], xs[..., o2:o3]
    xw, xa, xg = xs[..., o3:o4], xs[..., o4:o5], xs[..., o5:]
    w_log = -jax.nn.softplus(-(lp['w0'] + jnp.tanh(xw) @ lp['w_w2'])) - 0.5
    lw = -jnp.exp(w_log)
    a = jax.nn.sigmoid(lp['a0'] + xa @ lp['w_a2'])
    g = jax.nn.sigmoid(xg) @ lp['w_g2']
    hs = (B, T, RWKV_HEADS, RWKV_HD)
    kk = (k * lp['k_k']).reshape(hs)
    kk = (kk / jnp.maximum(jnp.sqrt(jnp.sum(kk * kk, axis=-1, keepdims=True)), 1e-12)).reshape(B, T, RWKV_W)
    k = k * (1.0 + (a - 1.0) * lp['k_a'])

    Tp = -(-T // RWKV_CHUNK) * RWKV_CHUNK
    pad = lambda x: jnp.pad(x, ((0, 0), (0, Tp - T), (0, 0)))
    y, s_end = _rwkv_scan(pad(r), pad(lw), pad(k), pad(v), pad(kk), pad(a), _state_to_stacked(wkv0))
    y = y[:, :T].reshape(hs)
    S_fin = _stacked_to_state(s_end)

    mu = jnp.mean(y, axis=-1, keepdims=True)
    var = jnp.mean((y - mu) ** 2, axis=-1, keepdims=True)
    yn = ((y - mu) * lax.rsqrt(var + GN_EPS)).reshape(B, T, RWKV_W) * lp['lnx_w'] + lp['lnx_b']
    rh, kh, vh = r.reshape(hs), k.reshape(hs), v.reshape(hs)
    bonus = (jnp.sum(rh * kh * lp['r_k'], axis=-1, keepdims=True) * vh).reshape(B, T, RWKV_W)
    return (yn + bonus) * g, p[:, -1], S_fin


def _cross_attn_core(q, mk, mv):
    B, T, _ = q.shape
    q = q.reshape(B, T, MEM_HEADS, MEM_HD)
    s = jnp.einsum('bthd,bmhd->bhtm', q, mk).astype(f32) * (MEM_HD ** -0.5)
    p = jax.nn.softmax(s, axis=-1)
    o = jnp.einsum('bhtm,bmhd->bthd', p, mv)
    return o.reshape(B, T, MEM_HEADS * MEM_HD)


def _peer_route_kernel(h_ref, g_ref, wq_ref, keys_ref, xn_ref, r2_ref, e2_ref, e1_ref, wd_ref,
                       q_scr, top_scr, cand_scr):
    x = h_ref[...]
    xn = x * lax.rsqrt(jnp.mean(x * x, axis=-1, keepdims=True) + NORM_EPS) * g_ref[...]
    xb = xn.astype(bf16)
    xn_ref[...] = xb
    q_scr[...] = lax.dot_general(xb, wq_ref[...], _NN, preferred_element_type=f32)
    neg = -jnp.inf
    bt = x.shape[0]
    key_id = lax.broadcasted_iota(jnp.int32, (N_KEYS, bt), 0)
    cand_id = lax.broadcasted_iota(jnp.int32, (PEER_TOPK * PEER_TOPK, bt), 0)

    def pop_max(x_, ids):
        m = jnp.max(x_, axis=0, keepdims=True)
        hit = ids == jnp.min(jnp.where(x_ == m, ids, ids.shape[0]), axis=0, keepdims=True)
        return m, hit, jnp.where(hit, neg, x_)

    def per_head(h, carry):
        ranks = []
        scores = []
        for c in range(2):
            off = pl.multiple_of((h * 2 + c) * (PEER_DK // 2), LANES)
            qhc = q_scr[:, pl.ds(off, PEER_DK // 2)].astype(bf16)
            s = lax.dot_general(keys_ref[h, c], qhc, _NT, preferred_element_type=f32)
            scores.append(s)
            x_ = s
            rank = jnp.full(s.shape, PEER_TOPK, f32)
            for i in range(PEER_TOPK):
                m, hit, x_ = pop_max(x_, key_id)
                top_scr[c, i:i + 1, :] = m
                rank = jnp.where(hit, jnp.float32(i), rank)
            ranks.append(rank)
        s1, s2 = scores
        for i in range(PEER_TOPK):
            cand_scr[i * PEER_TOPK:(i + 1) * PEER_TOPK, :] = top_scr[0, i:i + 1, :] + top_scr[1]
        best = top_scr[0, 0:1, :] + top_scr[1, 0:1, :]
        x_ = cand_scr[...]
        z = jnp.zeros_like(best)
        for i in range(PEER_TOPK):
            m, _, x_ = pop_max(x_, cand_id)
            z = z + jnp.exp(m - best)
        taken = (x_ == neg).astype(f32)
        width = jnp.zeros(s1.shape, f32)
        for i in range(PEER_TOPK):
            n_i = jnp.sum(taken[i * PEER_TOPK:(i + 1) * PEER_TOPK], axis=0, keepdims=True)
            width = jnp.where(ranks[0] == jnp.float32(i), n_i, width)
        r2_ref[h] = ranks[1]
        e2_ref[h] = jnp.exp(s2 - top_scr[1, 0:1, :])
        e1_ref[h] = jnp.exp(s1 - top_scr[0, 0:1, :]) / z
        wd_ref[h] = width
        return carry

    lax.fori_loop(0, PEER_HEADS, per_head, 0)


def _peer_route(h, g, wq_b, keys_b):
    T, D = h.shape
    bt = _peer_token_block(T)
    tab = jax.ShapeDtypeStruct((PEER_HEADS, N_KEYS, T), f32)
    tab_spec = pl.BlockSpec((PEER_HEADS, N_KEYS, bt), lambda i: (0, 0, i))
    return pl.pallas_call(
        _peer_route_kernel,
        grid=(T // bt,),
        in_specs=[pl.BlockSpec((bt, D), lambda i: (i, 0)), pl.BlockSpec((1, D), lambda i: (0, 0)),
                  pl.BlockSpec((D, PEER_HEADS * PEER_DK), lambda i: (0, 0), pipeline_mode=pl.Buffered(1)),
                  pl.BlockSpec((PEER_HEADS, 2, N_KEYS, PEER_DK // 2), lambda i: (0, 0, 0, 0))],
        out_specs=[pl.BlockSpec((bt, D), lambda i: (i, 0))] + [tab_spec] * 4,
        out_shape=[jax.ShapeDtypeStruct((T, D), bf16)] + [tab] * 4,
        scratch_shapes=[pltpu.VMEM((bt, PEER_HEADS * PEER_DK), f32), pltpu.VMEM((2, PEER_TOPK, bt), f32),
                        pltpu.VMEM((PEER_TOPK * PEER_TOPK, bt), f32)],
        compiler_params=pltpu.CompilerParams(dimension_semantics=("parallel",), vmem_limit_bytes=VMEM_LIMIT_BYTES),
        name="peer_route",
    )(h, g.reshape(1, D), wq_b, keys_b)


def _peer_mix_kernel(xn_ref, u_ref, v_ref, r2_ref, e2_ref, e1_ref, wd_ref, h_ref, g_ref, y_ref):
    j = pl.program_id(1)

    @pl.when(j == 0)
    def _():
        y_ref[...] = h_ref[...]

    ht = lax.dot_general(u_ref[...], xn_ref[...], _NT, preferred_element_type=f32)
    hid = 0.5 * ht * (1.0 + lax.erf(ht * (1.0 / math.sqrt(2.0))))
    rows = []
    for a in range(PEER_EXPERT_CHUNK // N_KEYS):
        g = None
        for h in range(PEER_HEADS):
            t = jnp.where(r2_ref[h] < wd_ref[h, a:a + 1, :], e2_ref[h] * e1_ref[h, a:a + 1, :], 0.0)
            g = t if g is None else g + t
        rows.append(g)
    gate = jnp.concatenate(rows, axis=0)
    w = (gate * hid).T.astype(bf16)
    y_ref[...] += lax.dot_general(w, v_ref[...], _NN, preferred_element_type=f32)

    @pl.when(j == pl.num_programs(1) - 1)
    def _():
        x = y_ref[...]
        y_ref[...] = x * lax.rsqrt(jnp.mean(x * x, axis=-1, keepdims=True) + NORM_EPS) * g_ref[...]


def _peer_mix(xn_b, u_b, v_b, r2, e2, e1, wd, h, final_g):
    T, D = h.shape
    E = u_b.shape[0]
    bt = _peer_token_block(T)
    EC = PEER_EXPERT_CHUNK
    tok1 = pl.BlockSpec((bt, D), lambda i, j: (i, 0), pipeline_mode=pl.Buffered(1))
    wts = pl.BlockSpec((EC, D), lambda i, j: (j, 0))
    full = pl.BlockSpec((PEER_HEADS, N_KEYS, bt), lambda i, j: (0, 0, i), pipeline_mode=pl.Buffered(1))
    part = pl.BlockSpec((PEER_HEADS, EC // N_KEYS, bt), lambda i, j: (0, j, i))
    return pl.pallas_call(
        _peer_mix_kernel,
        grid=(T // bt, E // EC),
        in_specs=[tok1, wts, wts, full, full, part, part, tok1, pl.BlockSpec((1, D), lambda i, j: (0, 0))],
        out_specs=pl.BlockSpec((bt, D), lambda i, j: (i, 0)),
        out_shape=jax.ShapeDtypeStruct((T, D), f32),
        compiler_params=pltpu.CompilerParams(dimension_semantics=("parallel", "arbitrary"),
                                             vmem_limit_bytes=VMEM_LIMIT_BYTES),
        name="peer_mix",
    )(xn_b, u_b, v_b, r2, e2, e1, wd, h, final_g.reshape(1, D))


def kernel(x_prompt, x_sample, cache_k, cache_v, state_wkv, state_shift, cache_mem_k, cache_mem_v, page_table, mem_prompt, norm1_g, w_in, lam_q1, lam_k1, lam_q2, lam_k2, subln_g, shift_mu, w0, w_w2, a0, w_a2, w_g2, k_k, k_a, r_k, lnx_w, lnx_b, w_out, norm2_g, mem_norm_g, w_cq, w_mk, w_mv, w_co, norm3_g, w_pq, peer_keys, peer_u, peer_v, final_g):
    l = 0
    D = D_MODEL
    Bp, Tp, _ = x_prompt.shape
    Bs, Ts, _ = x_sample.shape
    lp = dict(shift_mu=shift_mu[l], w0=w0[l], w_w2=w_w2[l], a0=a0[l], w_a2=w_a2[l], w_g2=w_g2[l], k_k=k_k[l],
              k_a=k_a[l], r_k=r_k[l], lnx_w=lnx_w[l], lnx_b=lnx_b[l])
    w_qkv = w_in[l][:, :3 * DIFF_W].astype(bf16)
    w_rwkv = w_in[l][:, 3 * DIFF_W:].astype(bf16)
    w_out_b, w_cq_b, w_co_b = w_out[l].astype(bf16), w_cq[l].astype(bf16), w_co[l].astype(bf16)
    w_mk_b, w_mv_b, w_pq_b = w_mk[l].astype(bf16), w_mv[l].astype(bf16), w_pq[l].astype(bf16)
    keys_b, u_b, v_b = peer_keys[l].astype(bf16), peer_u[l].astype(bf16), peer_v[l].astype(bf16)
    lam4 = jnp.stack([lam_q1[l], lam_k1[l], lam_q2[l], lam_k2[l]])
    lam = (jnp.exp(jnp.sum(lam_q1[l] * lam_k1[l])) - jnp.exp(jnp.sum(lam_q2[l] * lam_k2[l])) + LAM_INIT)

    xp = x_prompt.reshape(Bp * Tp, D)
    xs = x_sample.reshape(Bs * Ts, D)

    mem = mem_prompt.reshape(-1, D)
    mk_p = _matmul(mem, w_mk_b, gain=mem_norm_g[l], name="mem_k").reshape(Bp, -1, MEM_HEADS, MEM_HD)
    mv_p = _matmul(mem, w_mv_b, gain=mem_norm_g[l], name="mem_v").reshape(Bp, -1, MEM_HEADS, MEM_HD)

    qkv_p = _matmul(xp, w_qkv, gain=norm1_g[l], name="in_proj_attn")
    rw_p = _matmul(xp, w_rwkv, gain=norm1_g[l], name="in_proj_rwkv")
    a_p = _diff_attn_prompt(qkv_p, lam4, subln_g[l])
    r_p, sh_p, wkv_p = _rwkv7_mix(rw_p.reshape(Bp, Tp, RWKV_COLS), jnp.zeros((Bp, RWKV_COLS), f32),
                                  jnp.zeros((Bp, RWKV_HEADS, RWKV_HD, RWKV_HD), f32), lp)
    h_p = _matmul(jnp.concatenate([a_p, r_p.reshape(Bp * Tp, RWKV_W)], axis=-1), w_out_b, residual=xp, name="out_proj")

    qkv_s = _matmul(xs, w_qkv, gain=norm1_g[l], name="in_proj_attn")
    rw_s = _matmul(xs, w_rwkv, gain=norm1_g[l], name="in_proj_rwkv")
    q_s = qkv_s[:, :DIFF_W].reshape(Bs, Ts, DIFF_MAPS, DIFF_HD)
    k_s = qkv_s[:, DIFF_W:2 * DIFF_W].reshape(Bs, Ts, DIFF_MAPS, DIFF_HD)
    v_s = qkv_s[:, 2 * DIFF_W:].reshape(Bs, Ts, DIFF_HEADS, 2 * DIFF_HD)
    a_s = _diff_attn_sample(q_s, k_s, v_s, cache_k[l], cache_v[l], page_table, lam, subln_g[l])
    r_s, sh_s, wkv_s = _rwkv7_mix(rw_s.reshape(Bs, Ts, RWKV_COLS), state_shift[l], state_wkv[l], lp)
    h_s = _matmul(jnp.concatenate([a_s.reshape(Bs * Ts, DIFF_W), r_s.reshape(Bs * Ts, RWKV_W)], axis=-1), w_out_b,
                  residual=xs, name="out_proj")

    cq_p = _matmul(h_p, w_cq_b, gain=norm2_g[l], name="cross_q")
    cq_s = _matmul(h_s, w_cq_b, gain=norm2_g[l], name="cross_q")
    co_p = _cross_attn_core(cq_p.reshape(Bp, Tp, D), mk_p, mv_p).reshape(Bp * Tp, D)
    co_s = _cross_attn_core(cq_s.reshape(Bs, Ts, D), cache_mem_k[l], cache_mem_v[l]).reshape(Bs * Ts, D)
    h_p = _matmul(co_p, w_co_b, residual=h_p, name="cross_out")
    h_s = _matmul(co_s, w_co_b, residual=h_s, name="cross_out")

    h_all = jnp.concatenate([h_p, h_s], axis=0)
    xn_b, r2, e2, e1, wd = _peer_route(h_all, norm3_g[l], w_pq_b, keys_b)
    y_all = _peer_mix(xn_b, u_b, v_b, r2, e2, e1, wd, h_all, final_g)
    y_prompt = y_all[:Bp * Tp].reshape(Bp, Tp, D)
    y_sample = y_all[Bp * Tp:].reshape(Bs, Ts, D)

    k_p = qkv_p[:, DIFF_W:2 * DIFF_W].reshape(Bp, Tp, DIFF_MAPS, DIFF_HD)
    v_p = qkv_p[:, 2 * DIFF_W:].reshape(Bp, Tp, DIFF_HEADS, 2 * DIFF_HD)
    return (y_prompt, y_sample, k_p[None], v_p[None], k_s[None], v_s[None],
            wkv_p[None], sh_p[None], wkv_s[None], sh_s[None], mk_p[None], mv_p[None])
```

```python
import functools
import math

import jax
import jax.numpy as jnp
from jax import lax
from jax.experimental import pallas as pl
from jax.experimental.pallas import tpu as pltpu

f32, bf16 = jnp.float32, jnp.bfloat16

D_MODEL = 2048
DIFF_HD = 128
DIFF_HEADS = 4
DIFF_MAPS = 8
DIFF_W = 1024
SUBLN_EPS = 1e-5
LAM_INIT = 0.8 - 0.6 * math.exp(-0.3 * 0)
RWKV_HD = 64
RWKV_W = 1024
RWKV_HEADS = 16
DECAY_LORA = 96
ICLR_LORA = 96
GATE_LORA = 256
RWKV_COLS = 3 * RWKV_W + DECAY_LORA + ICLR_LORA + GATE_LORA
GN_EPS = 64e-5
MEM_HEADS = 4
MEM_HD = D_MODEL // MEM_HEADS
PEER_HEADS = 8
N_KEYS = 128
PEER_TOPK = 16
PEER_DK = 256
NORM_EPS = 1e-6

VMEM_LIMIT_BYTES = 60 << 20
LANES = 128

RWKV_CHUNK = 64
HEAD_PAIR_LANES = 2 * RWKV_HD
RWKV_PAIRS_PER_STEP = 4
FLASH_BLOCK = 512
SAMPLE_PAGES_PER_STEP = 4
PEER_TOKEN_BLOCK_MAX = 640
PEER_EXPERT_CHUNK = 1024
MATMUL_ROWS = 512


def _peer_token_block(n_tokens):
    return max(b for b in range(LANES, PEER_TOKEN_BLOCK_MAX + 1, LANES) if n_tokens % b == 0)

_NN = (((1,), (0,)), ((), ()))
_NT = (((1,), (1,)), ((), ()))


def _matmul_kernel(*refs, norm, residual):
    x_ref, refs = refs[0], refs[1:]
    if norm:
        g_ref, refs = refs[0], refs[1:]
    w_ref, refs = refs[0], refs[1:]
    if residual:
        r_ref, refs = refs[0], refs[1:]
    o_ref = refs[0]
    x = x_ref[...]
    if norm:
        x = x * lax.rsqrt(jnp.mean(x * x, axis=-1, keepdims=True) + NORM_EPS) * g_ref[...]
    y = lax.dot_general(x.astype(bf16), w_ref[...], _NN, preferred_element_type=f32)
    if residual:
        y = r_ref[...] + y
    o_ref[...] = y


def _matmul(x, w_b, gain=None, residual=None, name="matmul"):
    M, K = x.shape
    N = w_b.shape[1]
    bm = min(MATMUL_ROWS, M)
    operands = [x]
    specs = [pl.BlockSpec((bm, K), lambda i: (i, 0))]
    if gain is not None:
        operands.append(gain.reshape(1, K))
        specs.append(pl.BlockSpec((1, K), lambda i: (0, 0)))
    operands.append(w_b)
    specs.append(pl.BlockSpec((K, N), lambda i: (0, 0), pipeline_mode=pl.Buffered(1)))
    if residual is not None:
        operands.append(residual)
        specs.append(pl.BlockSpec((bm, N), lambda i: (i, 0)))
    return pl.pallas_call(
        functools.partial(_matmul_kernel, norm=gain is not None, residual=residual is not None),
        grid=(M // bm,),
        in_specs=specs,
        out_specs=pl.BlockSpec((bm, N), lambda i: (i, 0)),
        out_shape=jax.ShapeDtypeStruct((M, N), f32),
        compiler_params=pltpu.CompilerParams(dimension_semantics=("parallel",), vmem_limit_bytes=VMEM_LIMIT_BYTES),
        name=name,
    )(*operands)


def _diff_lambda(lam_ref):
    lq1, lk1, lq2, lk2 = (lam_ref[i:i + 1, :] for i in range(4))
    return (jnp.exp(jnp.sum(lq1 * lk1, axis=-1, keepdims=True))
            - jnp.exp(jnp.sum(lq2 * lk2, axis=-1, keepdims=True)) + LAM_INIT)


def _diff_finish(o0, o1, lam, g):
    o = o0 - lam * o1
    o = o * lax.rsqrt(jnp.mean(o * o, axis=-1, keepdims=True) + SUBLN_EPS) * g
    return o * (1.0 - LAM_INIT)


def _flash_prompt_kernel(q_ref, k_ref, v_ref, lam_ref, g_ref, o_ref, m_scr, l_scr, acc, bias_scr, *, blk):
    h = pl.program_id(0)
    i = pl.program_id(1)
    j = pl.program_id(2)

    @pl.when(j == 0)
    def _():
        m_scr[...] = jnp.full(m_scr.shape, -jnp.inf, f32)
        l_scr[...] = jnp.zeros(l_scr.shape, f32)
        acc[...] = jnp.zeros(acc.shape, f32)
        back = (lax.broadcasted_iota(jnp.int32, (blk, blk), 0)
                - lax.broadcasted_iota(jnp.int32, (blk, blk), 1)).astype(f32)
        for m in range(2):
            slope = jnp.exp2(-(2 * h + m + 1).astype(f32))
            bias_scr[m, 0] = -slope * back
            bias_scr[m, 1] = jnp.where(back >= 0, -slope * back, -jnp.inf)

    @pl.when(j <= i)
    def _():
        diag = (j == i).astype(jnp.int32)
        v = v_ref[...].astype(bf16)
        for m in range(2):
            slope = jnp.exp2(-(2 * h + m + 1).astype(f32))
            shift = -slope * ((i - j) * blk).astype(f32)
            q = (q_ref[:, m * DIFF_HD:(m + 1) * DIFF_HD] * (DIFF_HD ** -0.5)).astype(bf16)
            k = k_ref[:, m * DIFF_HD:(m + 1) * DIFF_HD].astype(bf16)
            s = lax.dot_general(q, k, _NT, preferred_element_type=f32) + bias_scr[m, diag]
            m_old = m_scr[m]
            m_new = jnp.maximum(m_old, jnp.max(s, axis=-1, keepdims=True) + shift)
            p = jnp.exp(s - (m_new[:, :1] - shift))
            alpha = jnp.exp(m_old - m_new)
            l_scr[m] = alpha * l_scr[m] + jnp.sum(p, axis=-1, keepdims=True)
            acc[m] = alpha[:, :1] * acc[m] + lax.dot_general(p.astype(bf16), v, _NN, preferred_element_type=f32)
            m_scr[m] = m_new

    @pl.when(j == i)
    def _():
        o0 = acc[0] / l_scr[0][:, :1]
        o1 = acc[1] / l_scr[1][:, :1]
        o_ref[...] = _diff_finish(o0, o1, _diff_lambda(lam_ref), g_ref[...])


def _diff_attn_prompt(qkv, lam4, subln_g):
    T = qkv.shape[0]
    blk = FLASH_BLOCK
    nb = T // blk
    W = 2 * DIFF_HD
    kv_idx = lambda off: (lambda h, i, j: (jnp.minimum(j, i), off + h))
    return pl.pallas_call(
        functools.partial(_flash_prompt_kernel, blk=blk),
        grid=(DIFF_HEADS, nb, nb),
        in_specs=[pl.BlockSpec((blk, W), lambda h, i, j: (i, h)),
                  pl.BlockSpec((blk, W), kv_idx(DIFF_HEADS)),
                  pl.BlockSpec((blk, W), kv_idx(2 * DIFF_HEADS)),
                  pl.BlockSpec((4, DIFF_HD), lambda h, i, j: (0, 0)),
                  pl.BlockSpec((1, W), lambda h, i, j: (0, 0))],
        out_specs=pl.BlockSpec((blk, W), lambda h, i, j: (i, h)),
        out_shape=jax.ShapeDtypeStruct((T, DIFF_HEADS * W), f32),
        scratch_shapes=[pltpu.VMEM((2, blk, LANES), f32), pltpu.VMEM((2, blk, LANES), f32),
                        pltpu.VMEM((2, blk, W), f32), pltpu.VMEM((2, 2, blk, blk), f32)],
        compiler_params=pltpu.CompilerParams(dimension_semantics=("parallel", "parallel", "arbitrary"),
                                             vmem_limit_bytes=VMEM_LIMIT_BYTES),
        name="diff_attn_prompt",
    )(qkv, qkv, qkv, lam4, subln_g.reshape(1, W))


def _paged_attn_kernel(pt_ref, q_ref, kn_ref, vn_ref, *refs, pages, page, n_new):
    k_refs, v_refs = refs[:pages], refs[pages:2 * pages]
    lam_ref, g_ref, o_ref, m_scr, l_scr, acc = refs[2 * pages:]
    j = pl.program_id(1)
    nj = pl.num_programs(1)
    rows = DIFF_HEADS * n_new
    span = pages * page
    past = nj * span

    @pl.when(j == 0)
    def _():
        m_scr[...] = jnp.full(m_scr.shape, -jnp.inf, f32)
        l_scr[...] = jnp.zeros(l_scr.shape, f32)
        acc[...] = jnp.zeros(acc.shape, f32)

    def update(w, s, v):
        m_old = m_scr[w]
        m_new = jnp.maximum(m_old, jnp.max(s, axis=-1, keepdims=True))
        p = jnp.exp(s - m_new[:, :1])
        alpha = jnp.exp(m_old - m_new)
        l_scr[w] = alpha * l_scr[w] + jnp.sum(p, axis=-1, keepdims=True)
        acc[w] = alpha[:, :1] * acc[w] + lax.dot_general(p.astype(bf16), v, _NN, preferred_element_type=f32)
        m_scr[w] = m_new

    def logits(w, k, n_keys, key_pos):
        row_head = lax.broadcasted_iota(jnp.int32, (rows, n_keys), 0) // n_new
        lane_head = lax.broadcasted_iota(jnp.int32, (rows, n_keys), 1) % DIFF_HEADS
        slope = jnp.exp2(-(2 * row_head + w + 1).astype(f32))
        q = (q_ref[w] * (DIFF_HD ** -0.5)).astype(bf16)
        s = lax.dot_general(q, k, _NT, preferred_element_type=f32) + slope * key_pos
        return jnp.where(row_head == lane_head, s, -jnp.inf)

    n_keys = span * DIFF_HEADS
    key_pos = ((lax.broadcasted_iota(jnp.int32, (1, n_keys), 1) // DIFF_HEADS) + (j * span - past)).astype(f32)
    v = jnp.concatenate([v_refs[p][...].reshape(page * DIFF_HEADS, 2 * DIFF_HD) for p in range(pages)],
                        axis=0).astype(bf16)
    for w in range(2):
        k = jnp.concatenate([k_refs[p][pl.ds(w, page * DIFF_HEADS, stride=2), :] for p in range(pages)],
                            axis=0).astype(bf16)
        update(w, logits(w, k, n_keys, key_pos), v)

    @pl.when(j == nj - 1)
    def _():
        new_keys = n_new * DIFF_HEADS
        key_tok = lax.broadcasted_iota(jnp.int32, (rows, new_keys), 1) // DIFF_HEADS
        causal = key_tok <= lax.broadcasted_iota(jnp.int32, (rows, new_keys), 0) % n_new
        vn = vn_ref[...].astype(bf16)
        for w in range(2):
            s = logits(w, kn_ref[w].astype(bf16), new_keys, key_tok[:1].astype(f32))
            update(w, jnp.where(causal, s, -jnp.inf), vn)
        o0 = acc[0] / l_scr[0][:, :1]
        o1 = acc[1] / l_scr[1][:, :1]
        o_ref[...] = _diff_finish(o0, o1, _diff_lambda(lam_ref), g_ref[...])


def _diff_attn_sample(q, k_new, v_new, cache_k, cache_v, page_table, lam4, subln_g):
    B, Ts = q.shape[:2]
    n_pages = page_table.shape[1]
    page = cache_k.shape[1]
    pages = SAMPLE_PAGES_PER_STEP
    rows = DIFF_HEADS * Ts
    by_parity = lambda x, order: jnp.transpose(x.reshape(B, Ts, DIFF_HEADS, 2, DIFF_HD), order).reshape(B, 2, rows, DIFF_HD)
    qw = by_parity(q, (0, 3, 2, 1, 4))
    kw = by_parity(k_new, (0, 3, 1, 2, 4))
    vn = v_new.reshape(B, rows, 2 * DIFF_HD)
    ck = cache_k.reshape(cache_k.shape[0], page * DIFF_MAPS, DIFF_HD)
    par = pl.BlockSpec((None, 2, rows, DIFF_HD), lambda b, j, pt: (b, 0, 0, 0))
    kspec = lambda p: pl.BlockSpec((None, page * DIFF_MAPS, DIFF_HD), lambda b, j, pt: (pt[b, j * pages + p], 0, 0))
    vspec = lambda p: pl.BlockSpec((None, page, DIFF_HEADS, 2 * DIFF_HD),
                                   lambda b, j, pt: (pt[b, j * pages + p], 0, 0, 0))
    grid_spec = pltpu.PrefetchScalarGridSpec(
        num_scalar_prefetch=1,
        grid=(B, n_pages // pages),
        in_specs=[par, par, pl.BlockSpec((None, rows, 2 * DIFF_HD), lambda b, j, pt: (b, 0, 0))]
                 + [kspec(p) for p in range(pages)] + [vspec(p) for p in range(pages)]
                 + [pl.BlockSpec((4, DIFF_HD), lambda b, j, pt: (0, 0)),
                    pl.BlockSpec((1, 2 * DIFF_HD), lambda b, j, pt: (0, 0))],
        out_specs=pl.BlockSpec((None, rows, 2 * DIFF_HD), lambda b, j, pt: (b, 0, 0)),
        scratch_shapes=[pltpu.VMEM((2, rows, LANES), f32), pltpu.VMEM((2, rows, LANES), f32),
                        pltpu.VMEM((2, rows, 2 * DIFF_HD), f32)],
    )
    o = pl.pallas_call(
        functools.partial(_paged_attn_kernel, pages=pages, page=page, n_new=Ts),
        grid_spec=grid_spec,
        out_shape=jax.ShapeDtypeStruct((B, rows, 2 * DIFF_HD), f32),
        compiler_params=pltpu.CompilerParams(dimension_semantics=("parallel", "arbitrary"),
                                             vmem_limit_bytes=VMEM_LIMIT_BYTES),
        name="diff_attn_sample",
    )(page_table, qw, kw, vn, *([ck] * pages), *([cache_v] * pages), lam4, subln_g.reshape(1, -1))
    return jnp.transpose(o.reshape(B, DIFF_HEADS, Ts, 2 * DIFF_HD), (0, 2, 1, 3))


def _split(x):
    hi = x.astype(bf16)
    lo = (x - hi.astype(f32)).astype(bf16)
    return hi, lo


def _dot3(a, b, dims=_NN):
    dg = lambda x, y: lax.dot_general(x, y, dims, preferred_element_type=f32)
    return dg(a[0], b[0]) + dg(a[0], b[1]) + dg(a[1], b[0])


def _rwkv_scan_kernel(r_ref, lw_ref, k_ref, v_ref, kk_ref, a_ref, s0_ref, y_ref, sT_ref, state, *, pairs):
    c = pl.program_id(2)

    @pl.when(c == 0)
    def _():
        state[...] = s0_ref[...]

    C = r_ref.shape[0]
    C2 = 2 * C
    row = lax.broadcasted_iota(jnp.int32, (C2, C2), 0)
    col = lax.broadcasted_iota(jnp.int32, (C2, C2), 1)
    tril_strict = col < row
    tril_incl = col <= row
    eye = row == col
    first_head = lax.broadcasted_iota(jnp.int32, (C, HEAD_PAIR_LANES), 1) < RWKV_HD
    cum = (lax.broadcasted_iota(jnp.int32, (C, C), 1) <= lax.broadcasted_iota(jnp.int32, (C, C), 0)).astype(bf16)

    def stack(x):
        return jnp.concatenate([jnp.where(first_head, x, 0.0), jnp.where(first_head, 0.0, x)], axis=0)

    for p in range(pairs):
        sl = slice(p * HEAD_PAIR_LANES, (p + 1) * HEAD_PAIR_LANES)
        lw = lw_ref[:, sl]
        h1 = lw.astype(bf16)
        r1 = lw - h1.astype(f32)
        h2 = r1.astype(bf16)
        h3 = (r1 - h2.astype(f32)).astype(bf16)
        dn = lambda x: lax.dot_general(cum, x, _NN, preferred_element_type=f32)
        cs = dn(h1) + dn(h2) + dn(h3)
        cl = cs[C - 1:C, :]
        kk = kk_ref[:, sl]
        kv = k_ref[:, sl]
        b = kk * a_ref[:, sl]
        e_neg = jnp.exp(-cs)
        e_end = jnp.exp(cl - cs)
        A_s = stack(-kk * jnp.exp(cs - lw))
        R_s = stack(r_ref[:, sl] * jnp.exp(cs))
        sB = _split(stack(b * e_neg))
        sK = _split(stack(kv * e_neg))
        sV = _split(stack(v_ref[:, sl]))
        sA = _split(A_s)
        sR = _split(R_s)
        BhT = stack(b * e_end).T
        KhT = stack(kv * e_end).T
        L = jnp.where(tril_strict, _dot3(sA, sB, _NT), 0.0)
        Mak = jnp.where(tril_strict, _dot3(sA, sK, _NT), 0.0)
        Mrb = jnp.where(tril_incl, _dot3(sR, sB, _NT), 0.0)
        Mrk = jnp.where(tril_incl, _dot3(sR, sK, _NT), 0.0)
        T = jnp.where(eye, 1.0, 0.0).astype(f32) + L
        Lp = L
        n = 2
        while n < C:
            sLp = _split(Lp)
            Lp = _dot3(sLp, sLp)
            T = T + _dot3(_split(T), _split(Lp))
            n *= 2
        X1 = _dot3(_split(Mak), sV)
        sPU = _split(_dot3(_split(T), _split(jnp.concatenate([A_s, X1], axis=1))))
        QY = _dot3(_split(Mrb), sPU)
        Q = R_s + QY[:, :HEAD_PAIR_LANES]
        Yl = QY[:, HEAD_PAIR_LANES:] + _dot3(_split(Mrk), sV)
        GH = _dot3(_split(BhT), sPU)
        G = jnp.where(eye, jnp.exp(cl), 0.0) + GH[:, :HEAD_PAIR_LANES]
        H = GH[:, HEAD_PAIR_LANES:] + _dot3(_split(KhT), sV)
        sS = _split(state[p])
        Y = _dot3(_split(Q), sS) + Yl
        state[p] = _dot3(_split(G), sS) + H
        y_ref[:, sl] = Y[:C] + Y[C:]

    @pl.when(c == pl.num_programs(2) - 1)
    def _():
        sT_ref[...] = state[...]


def _rwkv_scan(r, lw, k, v, kk, a, s0):
    B, T, W = r.shape
    C = RWKV_CHUNK
    pairs = RWKV_PAIRS_PER_STEP
    groups = W // HEAD_PAIR_LANES // pairs
    seq = pl.BlockSpec((None, C, pairs * HEAD_PAIR_LANES), lambda b_, g, c: (b_, c, g))
    st = pl.BlockSpec((None, pairs, HEAD_PAIR_LANES, HEAD_PAIR_LANES), lambda b_, g, c: (b_, g, 0, 0))
    return pl.pallas_call(
        functools.partial(_rwkv_scan_kernel, pairs=pairs),
        grid=(B, groups, T // C),
        in_specs=[seq] * 6 + [st],
        out_specs=[seq, st],
        out_shape=[jax.ShapeDtypeStruct((B, T, W), f32), jax.ShapeDtypeStruct(s0.shape, f32)],
        scratch_shapes=[pltpu.VMEM((pairs, HEAD_PAIR_LANES, HEAD_PAIR_LANES), f32)],
        compiler_params=pltpu.CompilerParams(dimension_semantics=("parallel", "parallel", "arbitrary")),
        name="rwkv_scan",
    )(r, lw, k, v, kk, a, s0)


def _state_to_stacked(S):
    B, H, N, _ = S.shape
    St = jnp.swapaxes(S, -1, -2).reshape(B, H // 2, 2, N, N)
    z = jnp.zeros_like(St[:, :, 0])
    top = jnp.concatenate([St[:, :, 0], z], axis=-1)
    bot = jnp.concatenate([z, St[:, :, 1]], axis=-1)
    return jnp.concatenate([top, bot], axis=-2)


def _stacked_to_state(Ss):
    B, P, _, _ = Ss.shape
    N = RWKV_HD
    St = jnp.stack([Ss[:, :, :N, :N], Ss[:, :, N:, N:]], axis=2).reshape(B, 2 * P, N, N)
    return jnp.swapaxes(St, -1, -2)


def _rwkv7_mix(p, shift_prev, wkv0, lp):
    B, T, _ = p.shape
    p_prev = jnp.concatenate([shift_prev[:, None, :], p[:, :-1]], axis=1)
    xs = p + (p_prev - p) * lp['shift_mu']
    o1, o2, o3 = RWKV_W, 2 * RWKV_W, 3 * RWKV_W
    o4 = o3 + DECAY_LORA
    o5 = o4 + ICLR_LORA
    r, k, v = xs[..., :o1], xs[..., o1:o2], xs[..., o2:o3]
    xw, xa, xg = xs[..., o3:o4], xs[..., o4:o5], xs[..., o5:]
    w_log = -jax.nn.softplus(-(lp['w0'] + jnp.tanh(xw) @ lp['w_w2'])) - 0.5
    lw = -jnp.exp(w_log)
    a = jax.nn.sigmoid(lp['a0'] + xa @ lp['w_a2'])
    g = jax.nn.sigmoid(xg) @ lp['w_g2']
    hs = (B, T, RWKV_HEADS, RWKV_HD)
    kk = (k * lp['k_k']).reshape(hs)
    kk = (kk / jnp.maximum(jnp.sqrt(jnp.sum(kk * kk, axis=-1, keepdims=True)), 1e-12)).reshape(B, T, RWKV_W)
    k = k * (1.0 + (a - 1.0) * lp['k_a'])

    Tp = -(-T // RWKV_CHUNK) * RWKV_CHUNK
    pad = lambda x: jnp.pad(x, ((0, 0), (0, Tp - T), (0, 0)))
    y, s_end = _rwkv_scan(pad(r), pad(lw), pad(k), pad(v), pad(kk), pad(a), _state_to_stacked(wkv0))
    y = y[:, :T].reshape(hs)
    S_fin = _stacked_to_state(s_end)

    mu = jnp.mean(y, axis=-1, keepdims=True)
    var = jnp.mean((y - mu) ** 2, axis=-1, keepdims=True)
    yn = ((y - mu) * lax.rsqrt(var + GN_EPS)).reshape(B, T, RWKV_W) * lp['lnx_w'] + lp['lnx_b']
    rh, kh, vh = r.reshape(hs), k.reshape(hs), v.reshape(hs)
    bonus = (jnp.sum(rh * kh * lp['r_k'], axis=-1, keepdims=True) * vh).reshape(B, T, RWKV_W)
    return (yn + bonus) * g, p[:, -1], S_fin


def _cross_attn_core(q, mk, mv):
    B, T, _ = q.shape
    q = q.reshape(B, T, MEM_HEADS, MEM_HD)
    s = jnp.einsum('bthd,bmhd->bhtm', q, mk).astype(f32) * (MEM_HD ** -0.5)
    p = jax.nn.softmax(s, axis=-1)
    o = jnp.einsum('bhtm,bmhd->bthd', p, mv)
    return o.reshape(B, T, MEM_HEADS * MEM_HD)


def _peer_route_kernel(h_ref, g_ref, wq_ref, keys_ref, xn_ref, r2_ref, e2_ref, e1_ref, wd_ref,
                       q_scr, top_scr, cand_scr):
    x = h_ref[...]
    xn = x * lax.rsqrt(jnp.mean(x * x, axis=-1, keepdims=True) + NORM_EPS) * g_ref[...]
    xb = xn.astype(bf16)
    xn_ref[...] = xb
    q_scr[...] = lax.dot_general(xb, wq_ref[...], _NN, preferred_element_type=f32)
    neg = -jnp.inf
    bt = x.shape[0]
    key_id = lax.broadcasted_iota(jnp.int32, (N_KEYS, bt), 0)
    cand_id = lax.broadcasted_iota(jnp.int32, (PEER_TOPK * PEER_TOPK, bt), 0)

    def pop_max(x_, ids):
        m = jnp.max(x_, axis=0, keepdims=True)
        hit = ids == jnp.min(jnp.where(x_ == m, ids, ids.shape[0]), axis=0, keepdims=True)
        return m, hit, jnp.where(hit, neg, x_)

    def per_head(h, carry):
        ranks = []
        scores = []
        for c in range(2):
            off = pl.multiple_of((h * 2 + c) * (PEER_DK // 2), LANES)
            qhc = q_scr[:, pl.ds(off, PEER_DK // 2)].astype(bf16)
            s = lax.dot_general(keys_ref[h, c], qhc, _NT, preferred_element_type=f32)
            scores.append(s)
            x_ = s
            rank = jnp.full(s.shape, PEER_TOPK, f32)
            for i in range(PEER_TOPK):
                m, hit, x_ = pop_max(x_, key_id)
                top_scr[c, i:i + 1, :] = m
                rank = jnp.where(hit, jnp.float32(i), rank)
            ranks.append(rank)
        s1, s2 = scores
        for i in range(PEER_TOPK):
            cand_scr[i * PEER_TOPK:(i + 1) * PEER_TOPK, :] = top_scr[0, i:i + 1, :] + top_scr[1]
        best = top_scr[0, 0:1, :] + top_scr[1, 0:1, :]
        x_ = cand_scr[...]
        z = jnp.zeros_like(best)
        for i in range(PEER_TOPK):
            m, _, x_ = pop_max(x_, cand_id)
            z = z + jnp.exp(m - best)
        taken = (x_ == neg).astype(f32)
        width = jnp.zeros(s1.shape, f32)
        for i in range(PEER_TOPK):
            n_i = jnp.sum(taken[i * PEER_TOPK:(i + 1) * PEER_TOPK], axis=0, keepdims=True)
            width = jnp.where(ranks[0] == jnp.float32(i), n_i, width)
        r2_ref[h] = ranks[1]
        e2_ref[h] = jnp.exp(s2 - top_scr[1, 0:1, :])
        e1_ref[h] = jnp.exp(s1 - top_scr[0, 0:1, :]) / z
        wd_ref[h] = width
        return carry

    lax.fori_loop(0, PEER_HEADS, per_head, 0)


def _peer_route(h, g, wq_b, keys_b):
    T, D = h.shape
    bt = _peer_token_block(T)
    tab = jax.ShapeDtypeStruct((PEER_HEADS, N_KEYS, T), f32)
    tab_spec = pl.BlockSpec((PEER_HEADS, N_KEYS, bt), lambda i: (0, 0, i))
    return pl.pallas_call(
        _peer_route_kernel,
        grid=(T // bt,),
        in_specs=[pl.BlockSpec((bt, D), lambda i: (i, 0)), pl.BlockSpec((1, D), lambda i: (0, 0)),
                  pl.BlockSpec((D, PEER_HEADS * PEER_DK), lambda i: (0, 0), pipeline_mode=pl.Buffered(1)),
                  pl.BlockSpec((PEER_HEADS, 2, N_KEYS, PEER_DK // 2), lambda i: (0, 0, 0, 0))],
        out_specs=[pl.BlockSpec((bt, D), lambda i: (i, 0))] + [tab_spec] * 4,
        out_shape=[jax.ShapeDtypeStruct((T, D), bf16)] + [tab] * 4,
        scratch_shapes=[pltpu.VMEM((bt, PEER_HEADS * PEER_DK), f32), pltpu.VMEM((2, PEER_TOPK, bt), f32),
                        pltpu.VMEM((PEER_TOPK * PEER_TOPK, bt), f32)],
        compiler_params=pltpu.CompilerParams(dimension_semantics=("parallel",), vmem_limit_bytes=VMEM_LIMIT_BYTES),
        name="peer_route",
    )(h, g.reshape(1, D), wq_b, keys_b)


def _peer_mix_kernel(xn_ref, u_ref, v_ref, r2_ref, e2_ref, e1_ref, wd_ref, h_ref, g_ref, y_ref):
    j = pl.program_id(1)

    @pl.when(j == 0)
    def _():
        y_ref[...] = h_ref[...]

    ht = lax.dot_general(u_ref[...], xn_ref[...], _NT, preferred_element_type=f32)
    hid = 0.5 * ht * (1.0 + lax.erf(ht * (1.0 / math.sqrt(2.0))))
    rows = []
    for a in range(PEER_EXPERT_CHUNK // N_KEYS):
        g = None
        for h in range(PEER_HEADS):
            t = jnp.where(r2_ref[h] < wd_ref[h, a:a + 1, :], e2_ref[h] * e1_ref[h, a:a + 1, :], 0.0)
            g = t if g is None else g + t
        rows.append(g)
    gate = jnp.concatenate(rows, axis=0)
    w = (gate * hid).T.astype(bf16)
    y_ref[...] += lax.dot_general(w, v_ref[...], _NN, preferred_element_type=f32)

    @pl.when(j == pl.num_programs(1) - 1)
    def _():
        x = y_ref[...]
        y_ref[...] = x * lax.rsqrt(jnp.mean(x * x, axis=-1, keepdims=True) + NORM_EPS) * g_ref[...]


def _peer_mix(xn_b, u_b, v_b, r2, e2, e1, wd, h, final_g):
    T, D = h.shape
    E = u_b.shape[0]
    bt = _peer_token_block(T)
    EC = PEER_EXPERT_CHUNK
    tok1 = pl.BlockSpec((bt, D), lambda i, j: (i, 0), pipeline_mode=pl.Buffered(1))
    wts = pl.BlockSpec((EC, D), lambda i, j: (j, 0))
    full = pl.BlockSpec((PEER_HEADS, N_KEYS, bt), lambda i, j: (0, 0, i), pipeline_mode=pl.Buffered(1))
    part = pl.BlockSpec((PEER_HEADS, EC // N_KEYS, bt), lambda i, j: (0, j, i))
    return pl.pallas_call(
        _peer_mix_kernel,
        grid=(T // bt, E // EC),
        in_specs=[tok1, wts, wts, full, full, part, part, tok1, pl.BlockSpec((1, D), lambda i, j: (0, 0))],
        out_specs=pl.BlockSpec((bt, D), lambda i, j: (i, 0)),
        out_shape=jax.ShapeDtypeStruct((T, D), f32),
        compiler_params=pltpu.CompilerParams(dimension_semantics=("parallel", "arbitrary"),
                                             vmem_limit_bytes=VMEM_LIMIT_BYTES),
        name="peer_mix",
    )(xn_b, u_b, v_b, r2, e2, e1, wd, h, final_g.reshape(1, D))


def kernel(x_prompt, x_sample, cache_k, cache_v, state_wkv, state_shift, cache_mem_k, cache_mem_v, page_table, mem_prompt, norm1_g, w_in, lam_q1, lam_k1, lam_q2, lam_k2, subln_g, shift_mu, w0, w_w2, a0, w_a2, w_g2, k_k, k_a, r_k, lnx_w, lnx_b, w_out, norm2_g, mem_norm_g, w_cq, w_mk, w_mv, w_co, norm3_g, w_pq, peer_keys, peer_u, peer_v, final_g):
    l = 0
    D = D_MODEL
    Bp, Tp, _ = x_prompt.shape
    Bs, Ts, _ = x_sample.shape
    lp = dict(shift_mu=shift_mu[l], w0=w0[l], w_w2=w_w2[l], a0=a0[l], w_a2=w_a2[l], w_g2=w_g2[l], k_k=k_k[l],
              k_a=k_a[l], r_k=r_k[l], lnx_w=lnx_w[l], lnx_b=lnx_b[l])
    w_qkv = w_in[l][:, :3 * DIFF_W].astype(bf16)
    w_rwkv = w_in[l][:, 3 * DIFF_W:].astype(bf16)
    w_out_b, w_cq_b, w_co_b = w_out[l].astype(bf16), w_cq[l].astype(bf16), w_co[l].astype(bf16)
    w_mk_b, w_mv_b, w_pq_b = w_mk[l].astype(bf16), w_mv[l].astype(bf16), w_pq[l].astype(bf16)
    keys_b, u_b, v_b = peer_keys[l].astype(bf16), peer_u[l].astype(bf16), peer_v[l].astype(bf16)
    lam4 = jnp.stack([lam_q1[l], lam_k1[l], lam_q2[l], lam_k2[l]])

    xp = x_prompt.reshape(Bp * Tp, D)
    xs = x_sample.reshape(Bs * Ts, D)

    mem = mem_prompt.reshape(-1, D)
    mk_p = _matmul(mem, w_mk_b, gain=mem_norm_g[l], name="mem_k").reshape(Bp, -1, MEM_HEADS, MEM_HD)
    mv_p = _matmul(mem, w_mv_b, gain=mem_norm_g[l], name="mem_v").reshape(Bp, -1, MEM_HEADS, MEM_HD)

    qkv_p = _matmul(xp, w_qkv, gain=norm1_g[l], name="in_proj_attn")
    rw_p = _matmul(xp, w_rwkv, gain=norm1_g[l], name="in_proj_rwkv")
    a_p = _diff_attn_prompt(qkv_p, lam4, subln_g[l])
    r_p, sh_p, wkv_p = _rwkv7_mix(rw_p.reshape(Bp, Tp, RWKV_COLS), jnp.zeros((Bp, RWKV_COLS), f32),
                                  jnp.zeros((Bp, RWKV_HEADS, RWKV_HD, RWKV_HD), f32), lp)
    h_p = _matmul(jnp.concatenate([a_p, r_p.reshape(Bp * Tp, RWKV_W)], axis=-1), w_out_b, residual=xp, name="out_proj")

    qkv_s = _matmul(xs, w_qkv, gain=norm1_g[l], name="in_proj_attn")
    rw_s = _matmul(xs, w_rwkv, gain=norm1_g[l], name="in_proj_rwkv")
    q_s = qkv_s[:, :DIFF_W].reshape(Bs, Ts, DIFF_MAPS, DIFF_HD)
    k_s = qkv_s[:, DIFF_W:2 * DIFF_W].reshape(Bs, Ts, DIFF_MAPS, DIFF_HD)
    v_s = qkv_s[:, 2 * DIFF_W:].reshape(Bs, Ts, DIFF_HEADS, 2 * DIFF_HD)
    a_s = _diff_attn_sample(q_s, k_s, v_s, cache_k[l], cache_v[l], page_table, lam4, subln_g[l])
    r_s, sh_s, wkv_s = _rwkv7_mix(rw_s.reshape(Bs, Ts, RWKV_COLS), state_shift[l], state_wkv[l], lp)
    h_s = _matmul(jnp.concatenate([a_s.reshape(Bs * Ts, DIFF_W), r_s.reshape(Bs * Ts, RWKV_W)], axis=-1), w_out_b,
                  residual=xs, name="out_proj")

    cq_p = _matmul(h_p, w_cq_b, gain=norm2_g[l], name="cross_q")
    cq_s = _matmul(h_s, w_cq_b, gain=norm2_g[l], name="cross_q")
    co_p = _cross_attn_core(cq_p.reshape(Bp, Tp, D), mk_p, mv_p).reshape(Bp * Tp, D)
    co_s = _cross_attn_core(cq_s.reshape(Bs, Ts, D), cache_mem_k[l], cache_mem_v[l]).reshape(Bs * Ts, D)
    h_p = _matmul(co_p, w_co_b, residual=h_p, name="cross_out")
    h_s = _matmul(co_s, w_co_b, residual=h_s, name="cross_out")

    h_all = jnp.concatenate([h_p, h_s], axis=0)
    xn_b, r2, e2, e1, wd = _peer_route(h_all, norm3_g[l], w_pq_b, keys_b)
    y_all = _peer_mix(xn_b, u_b, v_b, r2, e2, e1, wd, h_all, final_g)
    y_prompt = y_all[:Bp * Tp].reshape(Bp, Tp, D)
    y_sample = y_all[Bp * Tp:].reshape(Bs, Ts, D)

    k_p = qkv_p[:, DIFF_W:2 * DIFF_W].reshape(Bp, Tp, DIFF_MAPS, DIFF_HD)
    v_p = qkv_p[:, 2 * DIFF_W:].reshape(Bp, Tp, DIFF_HEADS, 2 * DIFF_HD)
    return (y_prompt, y_sample, k_p[None], v_p[None], k_s[None], v_s[None],
            wkv_p[None], sh_p[None], wkv_s[None], sh_s[None], mk_p[None], mv_p[None])
```

```python
import functools
import math

import jax
import jax.numpy as jnp
from jax import lax
from jax.experimental import pallas as pl
from jax.experimental.pallas import tpu as pltpu

f32, bf16 = jnp.float32, jnp.bfloat16

D_MODEL = 2048
DIFF_HD = 128
DIFF_HEADS = 4
DIFF_MAPS = 8
DIFF_W = 1024
SUBLN_EPS = 1e-5
LAM_INIT = 0.8 - 0.6 * math.exp(-0.3 * 0)
RWKV_HD = 64
RWKV_W = 1024
RWKV_HEADS = 16
DECAY_LORA = 96
ICLR_LORA = 96
GATE_LORA = 256
RWKV_COLS = 3 * RWKV_W + DECAY_LORA + ICLR_LORA + GATE_LORA
GN_EPS = 64e-5
MEM_HEADS = 4
MEM_HD = D_MODEL // MEM_HEADS
PEER_HEADS = 8
N_KEYS = 128
PEER_TOPK = 16
PEER_DK = 256
NORM_EPS = 1e-6

VMEM_LIMIT_BYTES = 60 << 20
LANES = 128
SUBLANES = 8

RWKV_CHUNK = 64
HEAD_PAIR_LANES = 2 * RWKV_HD
RWKV_ROWS = 256
FLASH_BLOCK = 512
SAMPLE_PAGES_PER_STEP = 4
CROSS_ROWS = 512
PEER_TOKEN_BLOCK_MAX = 640
PEER_EXPERT_CHUNK = 1024
PEER_MIX_SLICES = 2
MATMUL_ROWS = 512

LORA_LO = 3 * RWKV_W
LORA_WIN = 2 * LANES
GATE_LO = RWKV_COLS - GATE_LORA - RWKV_HD
GATE_WIN = RWKV_COLS - GATE_LO


def _peer_token_block(n_tokens):
    return max(b for b in range(LANES, PEER_TOKEN_BLOCK_MAX + 1, LANES) if n_tokens % b == 0)


_NN = (((1,), (0,)), ((), ()))
_NT = (((1,), (1,)), ((), ()))


def _matmul_kernel(*refs, norm, residual):
    x_ref, refs = refs[0], refs[1:]
    if norm:
        g_ref, refs = refs[0], refs[1:]
    w_ref, refs = refs[0], refs[1:]
    if residual:
        r_ref, refs = refs[0], refs[1:]
    o_ref = refs[0]
    x = x_ref[...]
    if norm:
        x = x * lax.rsqrt(jnp.mean(x * x, axis=-1, keepdims=True) + NORM_EPS) * g_ref[...]
    y = lax.dot_general(x.astype(bf16), w_ref[...], _NN, preferred_element_type=f32)
    if residual:
        y = r_ref[...] + y
    o_ref[...] = y


def _matmul(x, w_b, gain=None, residual=None, name="matmul"):
    M, K = x.shape
    N = w_b.shape[1]
    bm = min(MATMUL_ROWS, M)
    operands = [x]
    specs = [pl.BlockSpec((bm, K), lambda i: (i, 0))]
    if gain is not None:
        operands.append(gain.reshape(1, K))
        specs.append(pl.BlockSpec((1, K), lambda i: (0, 0)))
    operands.append(w_b)
    specs.append(pl.BlockSpec((K, N), lambda i: (0, 0), pipeline_mode=pl.Buffered(1)))
    if residual is not None:
        operands.append(residual)
        specs.append(pl.BlockSpec((bm, N), lambda i: (i, 0)))
    return pl.pallas_call(
        functools.partial(_matmul_kernel, norm=gain is not None, residual=residual is not None),
        grid=(M // bm,),
        in_specs=specs,
        out_specs=pl.BlockSpec((bm, N), lambda i: (i, 0)),
        out_shape=jax.ShapeDtypeStruct((M, N), f32),
        compiler_params=pltpu.CompilerParams(dimension_semantics=("parallel",), vmem_limit_bytes=VMEM_LIMIT_BYTES),
        name=name,
    )(*operands)


def _diff_lambda(lam_ref):
    lq1, lk1, lq2, lk2 = (lam_ref[i:i + 1, :] for i in range(4))
    return (jnp.exp(jnp.sum(lq1 * lk1, axis=-1, keepdims=True))
            - jnp.exp(jnp.sum(lq2 * lk2, axis=-1, keepdims=True)) + LAM_INIT)


def _diff_finish(o0, o1, lam, g):
    o = o0 - lam * o1
    o = o * lax.rsqrt(jnp.mean(o * o, axis=-1, keepdims=True) + SUBLN_EPS) * g
    return o * (1.0 - LAM_INIT)


def _flash_prompt_kernel(q_ref, k_ref, v_ref, lam_ref, g_ref, o_ref, m_scr, l_scr, acc, bias_scr, *, blk):
    h = pl.program_id(0)
    i = pl.program_id(1)
    j = pl.program_id(2)

    @pl.when(j == 0)
    def _():
        m_scr[...] = jnp.full(m_scr.shape, -jnp.inf, f32)
        l_scr[...] = jnp.zeros(l_scr.shape, f32)
        acc[...] = jnp.zeros(acc.shape, f32)
        back = (lax.broadcasted_iota(jnp.int32, (blk, blk), 0)
                - lax.broadcasted_iota(jnp.int32, (blk, blk), 1)).astype(f32)
        for m in range(2):
            slope = jnp.exp2(-(2 * h + m + 1).astype(f32))
            bias_scr[m, 0] = -slope * back
            bias_scr[m, 1] = jnp.where(back >= 0, -slope * back, -jnp.inf)

    @pl.when(j <= i)
    def _():
        diag = (j == i).astype(jnp.int32)
        v = v_ref[...].astype(bf16)
        for m in range(2):
            slope = jnp.exp2(-(2 * h + m + 1).astype(f32))
            shift = -slope * ((i - j) * blk).astype(f32)
            q = (q_ref[:, m * DIFF_HD:(m + 1) * DIFF_HD] * (DIFF_HD ** -0.5)).astype(bf16)
            k = k_ref[:, m * DIFF_HD:(m + 1) * DIFF_HD].astype(bf16)
            s = lax.dot_general(q, k, _NT, preferred_element_type=f32) + bias_scr[m, diag]
            m_old = m_scr[m]
            m_new = jnp.maximum(m_old, jnp.max(s, axis=-1, keepdims=True) + shift)
            p = jnp.exp(s - (m_new[:, :1] - shift))
            alpha = jnp.exp(m_old - m_new)
            l_scr[m] = alpha * l_scr[m] + jnp.sum(p, axis=-1, keepdims=True)
            acc[m] = alpha[:, :1] * acc[m] + lax.dot_general(p.astype(bf16), v, _NN, preferred_element_type=f32)
            m_scr[m] = m_new

    @pl.when(j == i)
    def _():
        o0 = acc[0] / l_scr[0][:, :1]
        o1 = acc[1] / l_scr[1][:, :1]
        o_ref[...] = _diff_finish(o0, o1, _diff_lambda(lam_ref), g_ref[...])


def _diff_attn_prompt(qkv, lam4, subln_g):
    T = qkv.shape[0]
    blk = FLASH_BLOCK
    nb = T // blk
    W = 2 * DIFF_HD
    kv_idx = lambda off: (lambda h, i, j: (jnp.minimum(j, i), off + h))
    return pl.pallas_call(
        functools.partial(_flash_prompt_kernel, blk=blk),
        grid=(DIFF_HEADS, nb, nb),
        in_specs=[pl.BlockSpec((blk, W), lambda h, i, j: (i, h)),
                  pl.BlockSpec((blk, W), kv_idx(DIFF_HEADS)),
                  pl.BlockSpec((blk, W), kv_idx(2 * DIFF_HEADS)),
                  pl.BlockSpec((4, DIFF_HD), lambda h, i, j: (0, 0)),
                  pl.BlockSpec((1, W), lambda h, i, j: (0, 0))],
        out_specs=pl.BlockSpec((blk, W), lambda h, i, j: (i, h)),
        out_shape=jax.ShapeDtypeStruct((T, DIFF_HEADS * W), f32),
        scratch_shapes=[pltpu.VMEM((2, blk, LANES), f32), pltpu.VMEM((2, blk, LANES), f32),
                        pltpu.VMEM((2, blk, W), f32), pltpu.VMEM((2, 2, blk, blk), f32)],
        compiler_params=pltpu.CompilerParams(dimension_semantics=("parallel", "parallel", "arbitrary"),
                                             vmem_limit_bytes=VMEM_LIMIT_BYTES),
        name="diff_attn_prompt",
    )(qkv, qkv, qkv, lam4, subln_g.reshape(1, W))


def _paged_attn_kernel(pt_ref, q_ref, kn_ref, vn_ref, *refs, pages, page, n_new):
    k_refs, v_refs = refs[:pages], refs[pages:2 * pages]
    lam_ref, g_ref, o_ref, m_scr, l_scr, acc = refs[2 * pages:]
    j = pl.program_id(1)
    nj = pl.num_programs(1)
    rows = DIFF_HEADS * n_new
    span = pages * page
    past = nj * span

    @pl.when(j == 0)
    def _():
        m_scr[...] = jnp.full(m_scr.shape, -jnp.inf, f32)
        l_scr[...] = jnp.zeros(l_scr.shape, f32)
        acc[...] = jnp.zeros(acc.shape, f32)

    def update(w, s, v):
        m_old = m_scr[w]
        m_new = jnp.maximum(m_old, jnp.max(s, axis=-1, keepdims=True))
        p = jnp.exp(s - m_new[:, :1])
        alpha = jnp.exp(m_old - m_new)
        l_scr[w] = alpha * l_scr[w] + jnp.sum(p, axis=-1, keepdims=True)
        acc[w] = alpha[:, :1] * acc[w] + lax.dot_general(p.astype(bf16), v, _NN, preferred_element_type=f32)
        m_scr[w] = m_new

    def logits(w, k, n_keys, key_pos):
        row_head = lax.broadcasted_iota(jnp.int32, (rows, n_keys), 0) // n_new
        lane_head = lax.broadcasted_iota(jnp.int32, (rows, n_keys), 1) % DIFF_HEADS
        slope = jnp.exp2(-(2 * row_head + w + 1).astype(f32))
        q = (q_ref[w] * (DIFF_HD ** -0.5)).astype(bf16)
        s = lax.dot_general(q, k, _NT, preferred_element_type=f32) + slope * key_pos
        return jnp.where(row_head == lane_head, s, -jnp.inf)

    n_keys = span * DIFF_HEADS
    key_pos = ((lax.broadcasted_iota(jnp.int32, (1, n_keys), 1) // DIFF_HEADS) + (j * span - past)).astype(f32)
    v = jnp.concatenate([v_refs[p][...].reshape(page * DIFF_HEADS, 2 * DIFF_HD) for p in range(pages)],
                        axis=0).astype(bf16)
    for w in range(2):
        k = jnp.concatenate([k_refs[p][pl.ds(w, page * DIFF_HEADS, stride=2), :] for p in range(pages)],
                            axis=0).astype(bf16)
        update(w, logits(w, k, n_keys, key_pos), v)

    @pl.when(j == nj - 1)
    def _():
        new_keys = n_new * DIFF_HEADS
        key_tok = lax.broadcasted_iota(jnp.int32, (rows, new_keys), 1) // DIFF_HEADS
        causal = key_tok <= lax.broadcasted_iota(jnp.int32, (rows, new_keys), 0) % n_new
        vn = vn_ref[...].astype(bf16)
        for w in range(2):
            s = logits(w, kn_ref[w].astype(bf16), new_keys, key_tok[:1].astype(f32))
            update(w, jnp.where(causal, s, -jnp.inf), vn)
        o0 = acc[0] / l_scr[0][:, :1]
        o1 = acc[1] / l_scr[1][:, :1]
        o_ref[...] = _diff_finish(o0, o1, _diff_lambda(lam_ref), g_ref[...])


def _diff_attn_sample(q, k_new, v_new, cache_k, cache_v, page_table, lam4, subln_g):
    B, Ts = q.shape[:2]
    n_pages = page_table.shape[1]
    page = cache_k.shape[1]
    pages = SAMPLE_PAGES_PER_STEP
    rows = DIFF_HEADS * Ts
    by_parity = lambda x, order: jnp.transpose(x.reshape(B, Ts, DIFF_HEADS, 2, DIFF_HD), order).reshape(B, 2, rows, DIFF_HD)
    qw = by_parity(q, (0, 3, 2, 1, 4))
    kw = by_parity(k_new, (0, 3, 1, 2, 4))
    vn = v_new.reshape(B, rows, 2 * DIFF_HD)
    ck = cache_k.reshape(cache_k.shape[0], page * DIFF_MAPS, DIFF_HD)
    par = pl.BlockSpec((None, 2, rows, DIFF_HD), lambda b, j, pt: (b, 0, 0, 0))
    kspec = lambda p: pl.BlockSpec((None, page * DIFF_MAPS, DIFF_HD), lambda b, j, pt: (pt[b, j * pages + p], 0, 0))
    vspec = lambda p: pl.BlockSpec((None, page, DIFF_HEADS, 2 * DIFF_HD),
                                   lambda b, j, pt: (pt[b, j * pages + p], 0, 0, 0))
    grid_spec = pltpu.PrefetchScalarGridSpec(
        num_scalar_prefetch=1,
        grid=(B, n_pages // pages),
        in_specs=[par, par, pl.BlockSpec((None, rows, 2 * DIFF_HD), lambda b, j, pt: (b, 0, 0))]
                 + [kspec(p) for p in range(pages)] + [vspec(p) for p in range(pages)]
                 + [pl.BlockSpec((4, DIFF_HD), lambda b, j, pt: (0, 0)),
                    pl.BlockSpec((1, 2 * DIFF_HD), lambda b, j, pt: (0, 0))],
        out_specs=pl.BlockSpec((None, rows, 2 * DIFF_HD), lambda b, j, pt: (b, 0, 0)),
        scratch_shapes=[pltpu.VMEM((2, rows, LANES), f32), pltpu.VMEM((2, rows, LANES), f32),
                        pltpu.VMEM((2, rows, 2 * DIFF_HD), f32)],
    )
    o = pl.pallas_call(
        functools.partial(_paged_attn_kernel, pages=pages, page=page, n_new=Ts),
        grid_spec=grid_spec,
        out_shape=jax.ShapeDtypeStruct((B, rows, 2 * DIFF_HD), f32),
        compiler_params=pltpu.CompilerParams(dimension_semantics=("parallel", "arbitrary"),
                                             vmem_limit_bytes=VMEM_LIMIT_BYTES),
        name="diff_attn_sample",
    )(page_table, qw, kw, vn, *([ck] * pages), *([cache_v] * pages), lam4, subln_g.reshape(1, -1))
    return jnp.transpose(o.reshape(B, DIFF_HEADS, Ts, 2 * DIFF_HD), (0, 2, 1, 3))


def _split(x):
    hi = x.astype(bf16)
    lo = (x - hi.astype(f32)).astype(bf16)
    return hi, lo


def _dot3(a, b, dims=_NN):
    dg = lambda x, y: lax.dot_general(x, y, dims, preferred_element_type=f32)
    return dg(a[0], b[0]) + dg(a[0], b[1]) + dg(a[1], b[0])


def _split3_dot(x, ones_b, ones_first):
    hi = x.astype(bf16)
    r1 = x - hi.astype(f32)
    mid = r1.astype(bf16)
    lo = (r1 - mid.astype(f32)).astype(bf16)
    if ones_first:
        dn = lambda y: lax.dot_general(ones_b, y, _NN, preferred_element_type=f32)
    else:
        dn = lambda y: lax.dot_general(y, ones_b, _NN, preferred_element_type=f32)
    return dn(hi) + dn(mid) + dn(lo)


def _head_sums(x):
    same_head = (lax.broadcasted_iota(jnp.int32, (HEAD_PAIR_LANES, HEAD_PAIR_LANES), 0) // RWKV_HD
                 == lax.broadcasted_iota(jnp.int32, (HEAD_PAIR_LANES, HEAD_PAIR_LANES), 1) // RWKV_HD).astype(bf16)
    return jnp.concatenate(
        [_split3_dot(x[:, p * HEAD_PAIR_LANES:(p + 1) * HEAD_PAIR_LANES], same_head, False)
         for p in range(x.shape[1] // HEAD_PAIR_LANES)], axis=1)


def _rwkv_pre_kernel(p_ref, sh_ref, mu_ref, w0_ref, a0_ref, kk_ref, ka_ref, ww_ref, wa_ref, wg_ref,
                     r_ref, lw_ref, k_ref, v_ref, kkn_ref, a_ref, g_ref, carry):
    t = pl.program_id(1)

    @pl.when(t == 0)
    def _():
        carry[...] = sh_ref[...]

    p = p_ref[...]
    rows = p.shape[0]
    prev = jnp.concatenate([carry[...], p[:rows - 1]], axis=0)
    carry[...] = p[rows - 1:rows]
    xs = p + (prev - p) * mu_ref[...]
    k = xs[:, RWKV_W:2 * RWKV_W]
    lora = xs[:, LORA_LO:LORA_LO + LORA_WIN]
    gate = xs[:, GATE_LO:]
    dn = lambda x, w_ref: lax.dot_general(x.astype(bf16), w_ref[...], _NN, preferred_element_type=f32)
    z = -(w0_ref[...] + dn(jnp.tanh(lora), ww_ref))
    w_log = -(jnp.maximum(z, 0.0) + jnp.log(1.0 + jnp.exp(-jnp.abs(z)))) - 0.5
    a = 1.0 / (1.0 + jnp.exp(-(a0_ref[...] + dn(lora, wa_ref))))
    kk = k * kk_ref[...]
    norm = jnp.maximum(jnp.sqrt(_head_sums(kk * kk)), 1e-12)
    r_ref[...] = xs[:, :RWKV_W]
    lw_ref[...] = -jnp.exp(w_log)
    k_ref[...] = k * (1.0 + (a - 1.0) * ka_ref[...])
    v_ref[...] = xs[:, 2 * RWKV_W:3 * RWKV_W]
    kkn_ref[...] = kk / norm
    a_ref[...] = a
    g_ref[...] = dn(1.0 / (1.0 + jnp.exp(-gate)), wg_ref)


def _rwkv_pre(p, shift_prev, lp):
    B, T, _ = p.shape
    bt = min(T, RWKV_ROWS)
    pad_rows = lambda w, lo, n: jnp.zeros((n, RWKV_W), f32).at[lo:lo + w.shape[0]].set(w).astype(bf16)
    ww = pad_rows(lp['w_w2'], 0, LORA_WIN)
    wa = pad_rows(lp['w_a2'], DECAY_LORA, LORA_WIN)
    wg = pad_rows(lp['w_g2'], GATE_WIN - GATE_LORA, GATE_WIN)
    vec = lambda x: x.reshape(1, -1)
    const = lambda shape: pl.BlockSpec(shape, lambda b, t: (0,) * len(shape))
    seq = pl.BlockSpec((None, bt, RWKV_W), lambda b, t: (b, t, 0))
    return pl.pallas_call(
        _rwkv_pre_kernel,
        grid=(B, T // bt),
        in_specs=[pl.BlockSpec((None, bt, RWKV_COLS), lambda b, t: (b, t, 0)),
                  pl.BlockSpec((None, 1, RWKV_COLS), lambda b, t: (b, 0, 0)),
                  const((1, RWKV_COLS))] + [const((1, RWKV_W))] * 4
                 + [const((LORA_WIN, RWKV_W)), const((LORA_WIN, RWKV_W)), const((GATE_WIN, RWKV_W))],
        out_specs=[seq] * 7,
        out_shape=[jax.ShapeDtypeStruct((B, T, RWKV_W), f32)] * 7,
        scratch_shapes=[pltpu.VMEM((1, RWKV_COLS), f32)],
        compiler_params=pltpu.CompilerParams(dimension_semantics=("parallel", "arbitrary"),
                                             vmem_limit_bytes=VMEM_LIMIT_BYTES),
        name="rwkv_pre",
    )(p, shift_prev.reshape(B, 1, RWKV_COLS), vec(lp['shift_mu']), vec(lp['w0']), vec(lp['a0']), vec(lp['k_k']),
      vec(lp['k_a']), ww, wa, wg)


def _rwkv_post_kernel(y_ref, r_ref, k_ref, v_ref, g_ref, lw_ref, lb_ref, rk_ref, o_ref):
    y = y_ref[...]
    inv = 1.0 / RWKV_HD
    d = y - _head_sums(y) * inv
    var = _head_sums(d * d) * inv
    yn = d * lax.rsqrt(var + GN_EPS) * lw_ref[...] + lb_ref[...]
    bonus = _head_sums(r_ref[...] * k_ref[...] * rk_ref[...]) * v_ref[...]
    o_ref[...] = (yn + bonus) * g_ref[...]


def _rwkv_post(y, r, k, v, g, lp):
    B, T, W = y.shape
    bt = min(T, RWKV_ROWS)
    seq = pl.BlockSpec((None, bt, W), lambda b, t: (b, t, 0))
    const = pl.BlockSpec((1, W), lambda b, t: (0, 0))
    return pl.pallas_call(
        _rwkv_post_kernel,
        grid=(B, T // bt),
        in_specs=[seq] * 5 + [const] * 3,
        out_specs=seq,
        out_shape=jax.ShapeDtypeStruct((B, T, W), f32),
        compiler_params=pltpu.CompilerParams(dimension_semantics=("parallel", "parallel")),
        name="rwkv_post",
    )(y, r, k, v, g, lp['lnx_w'].reshape(1, W), lp['lnx_b'].reshape(1, W), lp['r_k'].reshape(1, W))


def _rwkv_scan_kernel(r_ref, lw_ref, k_ref, v_ref, kk_ref, a_ref, s0_ref, y_ref, sT_ref, state, *, pairs):
    c = pl.program_id(1)

    @pl.when(c == 0)
    def _():
        state[...] = s0_ref[...]

    C = r_ref.shape[0]
    C2 = 2 * C
    row = lax.broadcasted_iota(jnp.int32, (C2, C2), 0)
    col = lax.broadcasted_iota(jnp.int32, (C2, C2), 1)
    tril_strict = col < row
    tril_incl = col <= row
    eye = row == col
    eye_keys = (lax.broadcasted_iota(jnp.int32, (HEAD_PAIR_LANES, HEAD_PAIR_LANES), 0)
                == lax.broadcasted_iota(jnp.int32, (HEAD_PAIR_LANES, HEAD_PAIR_LANES), 1))
    first_head = lax.broadcasted_iota(jnp.int32, (C, HEAD_PAIR_LANES), 1) < RWKV_HD
    cum = (lax.broadcasted_iota(jnp.int32, (C, C), 1) <= lax.broadcasted_iota(jnp.int32, (C, C), 0)).astype(bf16)

    def stack(x):
        return jnp.concatenate([jnp.where(first_head, x, 0.0), jnp.where(first_head, 0.0, x)], axis=0)

    P = range(pairs)
    sls = [slice(p * HEAD_PAIR_LANES, (p + 1) * HEAD_PAIR_LANES) for p in P]
    each = lambda fn, *ls: [fn(*xs) for xs in zip(*ls)]

    lw = [lw_ref[:, sl] for sl in sls]
    cs = each(lambda x: _split3_dot(x, cum, True), lw)
    cl = [x[C - 1:C, :] for x in cs]
    kk = [kk_ref[:, sl] for sl in sls]
    kv = [k_ref[:, sl] for sl in sls]
    b = each(lambda x, sl: x * a_ref[:, sl], kk, sls)
    e_neg = each(lambda x: jnp.exp(-x), cs)
    e_end = each(lambda x, y: jnp.exp(x - y), cl, cs)
    A_s = each(lambda x, c_, l_: stack(-x * jnp.exp(c_ - l_)), kk, cs, lw)
    R_s = each(lambda sl, c_: stack(r_ref[:, sl] * jnp.exp(c_)), sls, cs)
    sB = each(lambda x, e: _split(stack(x * e)), b, e_neg)
    sK = each(lambda x, e: _split(stack(x * e)), kv, e_neg)
    sV = each(lambda sl: _split(stack(v_ref[:, sl])), sls)
    sA = each(_split, A_s)
    sR = each(_split, R_s)
    BhT = each(lambda x, e: stack(x * e).T, b, e_end)
    KhT = each(lambda x, e: stack(x * e).T, kv, e_end)
    L = each(lambda x, y: jnp.where(tril_strict, _dot3(x, y, _NT), 0.0), sA, sB)
    Mak = each(lambda x, y: jnp.where(tril_strict, _dot3(x, y, _NT), 0.0), sA, sK)
    Mrb = each(lambda x, y: jnp.where(tril_incl, _dot3(x, y, _NT), 0.0), sR, sB)
    Mrk = each(lambda x, y: jnp.where(tril_incl, _dot3(x, y, _NT), 0.0), sR, sK)
    T = each(lambda x: jnp.where(eye, 1.0, 0.0).astype(f32) + x, L)
    Lp = L
    n = 2
    while n < C:
        sLp = each(_split, Lp)
        Lp = each(lambda x: _dot3(x, x), sLp)
        T = each(lambda t, lp_: t + _dot3(_split(t), _split(lp_)), T, Lp)
        n *= 2
    X1 = each(lambda m, v: _dot3(_split(m), v), Mak, sV)
    sPU = each(lambda t, a_, x: _split(_dot3(_split(t), _split(jnp.concatenate([a_, x], axis=1)))), T, A_s, X1)
    QY = each(lambda m, pu: _dot3(_split(m), pu), Mrb, sPU)
    Q = each(lambda r_, qy: r_ + qy[:, :HEAD_PAIR_LANES], R_s, QY)
    Yl = each(lambda qy, m, v: qy[:, HEAD_PAIR_LANES:] + _dot3(_split(m), v), QY, Mrk, sV)
    GH = each(lambda bt_, pu: _dot3(_split(bt_), pu), BhT, sPU)
    G = each(lambda c_, gh: jnp.where(eye_keys, jnp.exp(c_), 0.0) + gh[:, :HEAD_PAIR_LANES], cl, GH)
    H = each(lambda gh, kt, v: gh[:, HEAD_PAIR_LANES:] + _dot3(_split(kt), v), GH, KhT, sV)
    sS = [_split(state[p]) for p in P]
    Y = each(lambda q, s, yl: _dot3(_split(q), s) + yl, Q, sS, Yl)
    S_new = each(lambda g, s, h: _dot3(_split(g), s) + h, G, sS, H)
    for p in P:
        state[p] = S_new[p]
        y_ref[:, sls[p]] = Y[p][:C] + Y[p][C:]

    @pl.when(c == pl.num_programs(1) - 1)
    def _():
        sT_ref[...] = state[...]


def _rwkv_scan(r, lw, k, v, kk, a, s0, chunk):
    B, T, W = r.shape
    pairs = W // HEAD_PAIR_LANES
    seq = pl.BlockSpec((None, chunk, W), lambda b_, c: (b_, c, 0))
    st = pl.BlockSpec((None, pairs, HEAD_PAIR_LANES, HEAD_PAIR_LANES), lambda b_, c: (b_, 0, 0, 0))
    return pl.pallas_call(
        functools.partial(_rwkv_scan_kernel, pairs=pairs),
        grid=(B, T // chunk),
        in_specs=[seq] * 6 + [st],
        out_specs=[seq, st],
        out_shape=[jax.ShapeDtypeStruct((B, T, W), f32), jax.ShapeDtypeStruct(s0.shape, f32)],
        scratch_shapes=[pltpu.VMEM((pairs, HEAD_PAIR_LANES, HEAD_PAIR_LANES), f32)],
        compiler_params=pltpu.CompilerParams(dimension_semantics=("parallel", "arbitrary")),
        name="rwkv_scan",
    )(r, lw, k, v, kk, a, s0)


def _state_to_stacked(S):
    B, H, N, _ = S.shape
    St = jnp.swapaxes(S, -1, -2).reshape(B, H // 2, 2, N, N)
    z = jnp.zeros_like(St[:, :, 0])
    top = jnp.concatenate([St[:, :, 0], z], axis=-1)
    bot = jnp.concatenate([z, St[:, :, 1]], axis=-1)
    return jnp.concatenate([top, bot], axis=-2)


def _stacked_to_state(Ss):
    B, P, _, _ = Ss.shape
    N = RWKV_HD
    St = jnp.stack([Ss[:, :, :N, :N], Ss[:, :, N:, N:]], axis=2).reshape(B, 2 * P, N, N)
    return jnp.swapaxes(St, -1, -2)


def _rwkv7_mix(p, shift_prev, wkv0, lp):
    B, T, _ = p.shape
    r, lw, k, v, kk, a, g = _rwkv_pre(p, shift_prev, lp)
    chunk = min(RWKV_CHUNK, -(-T // SUBLANES) * SUBLANES)
    Tp = -(-T // chunk) * chunk
    pad = lambda x: jnp.pad(x, ((0, 0), (0, Tp - T), (0, 0)))
    y, s_end = _rwkv_scan(pad(r), pad(lw), pad(k), pad(v), pad(kk), pad(a), _state_to_stacked(wkv0), chunk)
    out = _rwkv_post(y[:, :T], r, k, v, g, lp)
    return out, p[:, -1], _stacked_to_state(s_end)


def _softmax_rows(s):
    e = jnp.exp(s - jnp.max(s, axis=-1, keepdims=True))
    return e / jnp.sum(e, axis=-1, keepdims=True)


def _cross_prompt_kernel(q_ref, k_ref, v_ref, o_ref):
    for h in range(MEM_HEADS):
        sl = slice(h * MEM_HD, (h + 1) * MEM_HD)
        s = lax.dot_general(q_ref[:, sl].astype(bf16), k_ref[:, sl].astype(bf16), _NT, preferred_element_type=f32)
        p = _softmax_rows(s * (MEM_HD ** -0.5))
        o_ref[:, sl] = lax.dot_general(p.astype(bf16), v_ref[:, sl].astype(bf16), _NN, preferred_element_type=f32)


def _cross_attn_prompt(q, mk, mv):
    T, D = q.shape
    M = mk.shape[0]
    bt = min(T, CROSS_ROWS)
    mem = pl.BlockSpec((M, D), lambda i: (0, 0))
    return pl.pallas_call(
        _cross_prompt_kernel,
        grid=(T // bt,),
        in_specs=[pl.BlockSpec((bt, D), lambda i: (i, 0)), mem, mem],
        out_specs=pl.BlockSpec((bt, D), lambda i: (i, 0)),
        out_shape=jax.ShapeDtypeStruct((T, D), f32),
        compiler_params=pltpu.CompilerParams(dimension_semantics=("parallel",), vmem_limit_bytes=VMEM_LIMIT_BYTES),
        name="cross_attn_prompt",
    )(q, mk, mv)


def _cross_sample_kernel(q_ref, k_ref, v_ref, o_ref):
    ts = q_ref.shape[0]
    n = k_ref.shape[0] * MEM_HEADS
    rows = MEM_HEADS * ts
    q = jnp.concatenate([q_ref[:, h * MEM_HD:(h + 1) * MEM_HD] for h in range(MEM_HEADS)], axis=0).astype(bf16)
    k = k_ref[...].reshape(n, MEM_HD).astype(bf16)
    v = v_ref[...].reshape(n, MEM_HD).astype(bf16)
    s = lax.dot_general(q, k, _NT, preferred_element_type=f32) * (MEM_HD ** -0.5)
    own = ((lax.broadcasted_iota(jnp.int32, (rows, n), 0) // ts)
           == (lax.broadcasted_iota(jnp.int32, (rows, n), 1) % MEM_HEADS))
    p = _softmax_rows(jnp.where(own, s, -jnp.inf))
    o = lax.dot_general(p.astype(bf16), v, _NN, preferred_element_type=f32)
    for h in range(MEM_HEADS):
        o_ref[:, h * MEM_HD:(h + 1) * MEM_HD] = o[h * ts:(h + 1) * ts]


def _cross_attn_sample(q, mk, mv):
    B, Ts, D = q.shape
    M = mk.shape[1]
    mem = pl.BlockSpec((None, M, MEM_HEADS, MEM_HD), lambda b: (b, 0, 0, 0))
    tok = pl.BlockSpec((None, Ts, D), lambda b: (b, 0, 0))
    return pl.pallas_call(
        _cross_sample_kernel,
        grid=(B,),
        in_specs=[tok, mem, mem],
        out_specs=tok,
        out_shape=jax.ShapeDtypeStruct((B, Ts, D), f32),
        compiler_params=pltpu.CompilerParams(dimension_semantics=("parallel",), vmem_limit_bytes=VMEM_LIMIT_BYTES),
        name="cross_attn_sample",
    )(q, mk, mv)


def _peer_route_kernel(h_ref, g_ref, wq_ref, keys_ref, xn_ref, r2_ref, e2_ref, e1_ref, wd_ref,
                       q_scr, top_scr, cand_scr):
    x = h_ref[...]
    xn = x * lax.rsqrt(jnp.mean(x * x, axis=-1, keepdims=True) + NORM_EPS) * g_ref[...]
    xb = xn.astype(bf16)
    xn_ref[...] = xb
    q_scr[...] = lax.dot_general(xb, wq_ref[...], _NN, preferred_element_type=f32)
    neg = -jnp.inf
    bt = x.shape[0]
    key_id = lax.broadcasted_iota(jnp.int32, (N_KEYS, bt), 0)
    cand_id = lax.broadcasted_iota(jnp.int32, (PEER_TOPK * PEER_TOPK, bt), 0)

    def pop_max(x_, ids):
        m = jnp.max(x_, axis=0, keepdims=True)
        hit = ids == jnp.min(jnp.where(x_ == m, ids, ids.shape[0]), axis=0, keepdims=True)
        return m, hit, jnp.where(hit, neg, x_)

    def per_head(h, carry):
        ranks = []
        scores = []
        for c in range(2):
            off = pl.multiple_of((h * 2 + c) * (PEER_DK // 2), LANES)
            qhc = q_scr[:, pl.ds(off, PEER_DK // 2)].astype(bf16)
            s = lax.dot_general(keys_ref[h, c], qhc, _NT, preferred_element_type=f32)
            scores.append(s)
            x_ = s
            rank = jnp.full(s.shape, PEER_TOPK, f32)
            for i in range(PEER_TOPK):
                m, hit, x_ = pop_max(x_, key_id)
                top_scr[c, i:i + 1, :] = m
                rank = jnp.where(hit, jnp.float32(i), rank)
            ranks.append(rank)
        s1, s2 = scores
        for i in range(PEER_TOPK):
            cand_scr[i * PEER_TOPK:(i + 1) * PEER_TOPK, :] = top_scr[0, i:i + 1, :] + top_scr[1]
        best = top_scr[0, 0:1, :] + top_scr[1, 0:1, :]
        x_ = cand_scr[...]
        z = jnp.zeros_like(best)
        for i in range(PEER_TOPK):
            m, _, x_ = pop_max(x_, cand_id)
            z = z + jnp.exp(m - best)
        taken = (x_ == neg).astype(f32)
        width = jnp.zeros(s1.shape, f32)
        for i in range(PEER_TOPK):
            n_i = jnp.sum(taken[i * PEER_TOPK:(i + 1) * PEER_TOPK], axis=0, keepdims=True)
            width = jnp.where(ranks[0] == jnp.float32(i), n_i, width)
        r2_ref[h] = ranks[1]
        e2_ref[h] = jnp.exp(s2 - top_scr[1, 0:1, :])
        e1_ref[h] = jnp.exp(s1 - top_scr[0, 0:1, :]) / z
        wd_ref[h] = width
        return carry

    lax.fori_loop(0, PEER_HEADS, per_head, 0)


def _peer_route(h, g, wq_b, keys_b):
    T, D = h.shape
    bt = _peer_token_block(T)
    tab = jax.ShapeDtypeStruct((PEER_HEADS, N_KEYS, T), f32)
    tab_spec = pl.BlockSpec((PEER_HEADS, N_KEYS, bt), lambda i: (0, 0, i))
    return pl.pallas_call(
        _peer_route_kernel,
        grid=(T // bt,),
        in_specs=[pl.BlockSpec((bt, D), lambda i: (i, 0)), pl.BlockSpec((1, D), lambda i: (0, 0)),
                  pl.BlockSpec((D, PEER_HEADS * PEER_DK), lambda i: (0, 0), pipeline_mode=pl.Buffered(1)),
                  pl.BlockSpec((PEER_HEADS, 2, N_KEYS, PEER_DK // 2), lambda i: (0, 0, 0, 0))],
        out_specs=[pl.BlockSpec((bt, D), lambda i: (i, 0))] + [tab_spec] * 4,
        out_shape=[jax.ShapeDtypeStruct((T, D), bf16)] + [tab] * 4,
        scratch_shapes=[pltpu.VMEM((bt, PEER_HEADS * PEER_DK), f32), pltpu.VMEM((2, PEER_TOPK, bt), f32),
                        pltpu.VMEM((PEER_TOPK * PEER_TOPK, bt), f32)],
        compiler_params=pltpu.CompilerParams(dimension_semantics=("parallel",), vmem_limit_bytes=VMEM_LIMIT_BYTES),
        name="peer_route",
    )(h, g.reshape(1, D), wq_b, keys_b)


def _peer_mix_kernel(xn_ref, u_ref, v_ref, r2_ref, e2_ref, e1_ref, wd_ref, h_ref, g_ref, y_ref):
    j = pl.program_id(1)

    @pl.when(j == 0)
    def _():
        y_ref[...] = h_ref[...]

    per = PEER_EXPERT_CHUNK // PEER_MIX_SLICES
    xn = xn_ref[...]
    hts = [lax.dot_general(u_ref[k * per:(k + 1) * per, :], xn, _NT, preferred_element_type=f32)
           for k in range(PEER_MIX_SLICES)]
    total = None
    for k in range(PEER_MIX_SLICES):
        ht = hts[k]
        hid = 0.5 * ht * (1.0 + lax.erf(ht * (1.0 / math.sqrt(2.0))))
        rows = []
        for a in range(k * per // N_KEYS, (k + 1) * per // N_KEYS):
            g = None
            for h in range(PEER_HEADS):
                t = jnp.where(r2_ref[h] < wd_ref[h, a:a + 1, :], e2_ref[h] * e1_ref[h, a:a + 1, :], 0.0)
                g = t if g is None else g + t
            rows.append(g)
        gate = jnp.concatenate(rows, axis=0)
        w = (gate * hid).T.astype(bf16)
        d = lax.dot_general(w, v_ref[k * per:(k + 1) * per, :], _NN, preferred_element_type=f32)
        total = d if total is None else total + d
    y_ref[...] += total

    @pl.when(j == pl.num_programs(1) - 1)
    def _():
        x = y_ref[...]
        y_ref[...] = x * lax.rsqrt(jnp.mean(x * x, axis=-1, keepdims=True) + NORM_EPS) * g_ref[...]


def _peer_mix(xn_b, u_b, v_b, r2, e2, e1, wd, h, final_g):
    T, D = h.shape
    E = u_b.shape[0]
    bt = _peer_token_block(T)
    EC = PEER_EXPERT_CHUNK
    tok1 = pl.BlockSpec((bt, D), lambda i, j: (i, 0), pipeline_mode=pl.Buffered(1))
    wts = pl.BlockSpec((EC, D), lambda i, j: (j, 0))
    full = pl.BlockSpec((PEER_HEADS, N_KEYS, bt), lambda i, j: (0, 0, i), pipeline_mode=pl.Buffered(1))
    part = pl.BlockSpec((PEER_HEADS, EC // N_KEYS, bt), lambda i, j: (0, j, i))
    return pl.pallas_call(
        _peer_mix_kernel,
        grid=(T // bt, E // EC),
        in_specs=[tok1, wts, wts, full, full, part, part, tok1, pl.BlockSpec((1, D), lambda i, j: (0, 0))],
        out_specs=pl.BlockSpec((bt, D), lambda i, j: (i, 0)),
        out_shape=jax.ShapeDtypeStruct((T, D), f32),
        compiler_params=pltpu.CompilerParams(dimension_semantics=("parallel", "arbitrary"),
                                             vmem_limit_bytes=VMEM_LIMIT_BYTES),
        name="peer_mix",
    )(xn_b, u_b, v_b, r2, e2, e1, wd, h, final_g.reshape(1, D))


def kernel(x_prompt, x_sample, cache_k, cache_v, state_wkv, state_shift, cache_mem_k, cache_mem_v, page_table, mem_prompt, norm1_g, w_in, lam_q1, lam_k1, lam_q2, lam_k2, subln_g, shift_mu, w0, w_w2, a0, w_a2, w_g2, k_k, k_a, r_k, lnx_w, lnx_b, w_out, norm2_g, mem_norm_g, w_cq, w_mk, w_mv, w_co, norm3_g, w_pq, peer_keys, peer_u, peer_v, final_g):
    l = 0
    D = D_MODEL
    Bp, Tp, _ = x_prompt.shape
    Bs, Ts, _ = x_sample.shape
    assert Bp == 1, "the prompt group is handled as one sequence"
    lp = dict(shift_mu=shift_mu[l], w0=w0[l], w_w2=w_w2[l], a0=a0[l], w_a2=w_a2[l], w_g2=w_g2[l], k_k=k_k[l],
              k_a=k_a[l], r_k=r_k[l], lnx_w=lnx_w[l], lnx_b=lnx_b[l])
    w_qkv = w_in[l][:, :3 * DIFF_W].astype(bf16)
    w_rwkv = w_in[l][:, 3 * DIFF_W:].astype(bf16)
    w_out_b, w_cq_b, w_co_b = w_out[l].astype(bf16), w_cq[l].astype(bf16), w_co[l].astype(bf16)
    w_mk_b, w_mv_b, w_pq_b = w_mk[l].astype(bf16), w_mv[l].astype(bf16), w_pq[l].astype(bf16)
    keys_b, u_b, v_b = peer_keys[l].astype(bf16), peer_u[l].astype(bf16), peer_v[l].astype(bf16)
    lam4 = jnp.stack([lam_q1[l], lam_k1[l], lam_q2[l], lam_k2[l]])

    xp = x_prompt.reshape(Bp * Tp, D)
    xs = x_sample.reshape(Bs * Ts, D)

    mem = mem_prompt.reshape(-1, D)
    mk_p = _matmul(mem, w_mk_b, gain=mem_norm_g[l], name="mem_k")
    mv_p = _matmul(mem, w_mv_b, gain=mem_norm_g[l], name="mem_v")

    qkv_p = _matmul(xp, w_qkv, gain=norm1_g[l], name="in_proj_attn")
    rw_p = _matmul(xp, w_rwkv, gain=norm1_g[l], name="in_proj_rwkv")
    a_p = _diff_attn_prompt(qkv_p, lam4, subln_g[l])
    r_p, sh_p, wkv_p = _rwkv7_mix(rw_p.reshape(Bp, Tp, RWKV_COLS), jnp.zeros((Bp, RWKV_COLS), f32),
                                  jnp.zeros((Bp, RWKV_HEADS, RWKV_HD, RWKV_HD), f32), lp)
    h_p = _matmul(jnp.concatenate([a_p, r_p.reshape(Bp * Tp, RWKV_W)], axis=-1), w_out_b, residual=xp, name="out_proj")

    qkv_s = _matmul(xs, w_qkv, gain=norm1_g[l], name="in_proj_attn")
    rw_s = _matmul(xs, w_rwkv, gain=norm1_g[l], name="in_proj_rwkv")
    q_s = qkv_s[:, :DIFF_W].reshape(Bs, Ts, DIFF_MAPS, DIFF_HD)
    k_s = qkv_s[:, DIFF_W:2 * DIFF_W].reshape(Bs, Ts, DIFF_MAPS, DIFF_HD)
    v_s = qkv_s[:, 2 * DIFF_W:].reshape(Bs, Ts, DIFF_HEADS, 2 * DIFF_HD)
    a_s = _diff_attn_sample(q_s, k_s, v_s, cache_k[l], cache_v[l], page_table, lam4, subln_g[l])
    r_s, sh_s, wkv_s = _rwkv7_mix(rw_s.reshape(Bs, Ts, RWKV_COLS), state_shift[l], state_wkv[l], lp)
    h_s = _matmul(jnp.concatenate([a_s.reshape(Bs * Ts, DIFF_W), r_s.reshape(Bs * Ts, RWKV_W)], axis=-1), w_out_b,
                  residual=xs, name="out_proj")

    cq_p = _matmul(h_p, w_cq_b, gain=norm2_g[l], name="cross_q")
    cq_s = _matmul(h_s, w_cq_b, gain=norm2_g[l], name="cross_q")
    co_p = _cross_attn_prompt(cq_p, mk_p, mv_p)
    co_s = _cross_attn_sample(cq_s.reshape(Bs, Ts, D), cache_mem_k[l], cache_mem_v[l]).reshape(Bs * Ts, D)
    h_p = _matmul(co_p, w_co_b, residual=h_p, name="cross_out")
    h_s = _matmul(co_s, w_co_b, residual=h_s, name="cross_out")

    h_all = jnp.concatenate([h_p, h_s], axis=0)
    xn_b, r2, e2, e1, wd = _peer_route(h_all, norm3_g[l], w_pq_b, keys_b)
    y_all = _peer_mix(xn_b, u_b, v_b, r2, e2, e1, wd, h_all, final_g)
    y_prompt = y_all[:Bp * Tp].reshape(Bp, Tp, D)
    y_sample = y_all[Bp * Tp:].reshape(Bs, Ts, D)

    k_p = qkv_p[:, DIFF_W:2 * DIFF_W].reshape(Bp, Tp, DIFF_MAPS, DIFF_HD)
    v_p = qkv_p[:, 2 * DIFF_W:].reshape(Bp, Tp, DIFF_HEADS, 2 * DIFF_HD)
    mem_shape = (Bp, -1, MEM_HEADS, MEM_HD)
    return (y_prompt, y_sample, k_p[None], v_p[None], k_s[None], v_s[None],
            wkv_p[None], sh_p[None], wkv_s[None], sh_s[None], mk_p.reshape(mem_shape)[None], mv_p.reshape(mem_shape)[None])
```

```python
import functools
import math

import jax
import jax.numpy as jnp
from jax import lax
from jax.experimental import pallas as pl
from jax.experimental.pallas import tpu as pltpu

f32, bf16 = jnp.float32, jnp.bfloat16

D_MODEL = 2048
DIFF_HD = 128
DIFF_HEADS = 4
DIFF_MAPS = 8
DIFF_W = 1024
SUBLN_EPS = 1e-5
LAM_INIT = 0.8 - 0.6 * math.exp(-0.3 * 0)
RWKV_HD = 64
RWKV_W = 1024
RWKV_HEADS = 16
DECAY_LORA = 96
ICLR_LORA = 96
GATE_LORA = 256
RWKV_COLS = 3 * RWKV_W + DECAY_LORA + ICLR_LORA + GATE_LORA
GN_EPS = 64e-5
MEM_HEADS = 4
MEM_HD = D_MODEL // MEM_HEADS
PEER_HEADS = 8
N_KEYS = 128
PEER_TOPK = 16
PEER_DK = 256
NORM_EPS = 1e-6

VMEM_LIMIT_BYTES = 60 << 20
LANES = 128
SUBLANES = 8

RWKV_CHUNK = 64
HEAD_PAIR_LANES = 2 * RWKV_HD
RWKV_ROWS = 256
FLASH_BLOCK = 512
SAMPLE_PAGES_PER_STEP = 8
LOG2E = math.log2(math.e)
CROSS_ROWS = 512
PEER_TOKEN_BLOCK_MAX = 640
PEER_EXPERT_CHUNK = 1024
PEER_MIX_SLICES = 2
MATMUL_ROWS = 512
CAND_OFFSETS = [sum(PEER_TOPK // (k + 1) for k in range(i)) for i in range(PEER_TOPK + 1)]
CAND_ROWS = -(-CAND_OFFSETS[PEER_TOPK] // SUBLANES) * SUBLANES

LORA_LO = 3 * RWKV_W
LORA_WIN = 2 * LANES
GATE_LO = RWKV_COLS - GATE_LORA - RWKV_HD
GATE_WIN = RWKV_COLS - GATE_LO


def _peer_token_block(n_tokens):
    return max(b for b in range(LANES, PEER_TOKEN_BLOCK_MAX + 1, LANES) if n_tokens % b == 0)


_NN = (((1,), (0,)), ((), ()))
_NT = (((1,), (1,)), ((), ()))


def _matmul_kernel(*refs, norm, residual):
    x_ref, refs = refs[0], refs[1:]
    if norm:
        g_ref, refs = refs[0], refs[1:]
    w_ref, refs = refs[0], refs[1:]
    if residual:
        r_ref, refs = refs[0], refs[1:]
    o_ref = refs[0]
    x = x_ref[...]
    if norm:
        x = x * lax.rsqrt(jnp.mean(x * x, axis=-1, keepdims=True) + NORM_EPS) * g_ref[...]
    y = lax.dot_general(x.astype(bf16), w_ref[...], _NN, preferred_element_type=f32)
    if residual:
        y = r_ref[...] + y
    o_ref[...] = y


def _matmul(x, w_b, gain=None, residual=None, name="matmul"):
    M, K = x.shape
    N = w_b.shape[1]
    bm = min(MATMUL_ROWS, M)
    operands = [x]
    specs = [pl.BlockSpec((bm, K), lambda i: (i, 0))]
    if gain is not None:
        operands.append(gain.reshape(1, K))
        specs.append(pl.BlockSpec((1, K), lambda i: (0, 0)))
    operands.append(w_b)
    specs.append(pl.BlockSpec((K, N), lambda i: (0, 0), pipeline_mode=pl.Buffered(1)))
    if residual is not None:
        operands.append(residual)
        specs.append(pl.BlockSpec((bm, N), lambda i: (i, 0)))
    return pl.pallas_call(
        functools.partial(_matmul_kernel, norm=gain is not None, residual=residual is not None),
        grid=(M // bm,),
        in_specs=specs,
        out_specs=pl.BlockSpec((bm, N), lambda i: (i, 0)),
        out_shape=jax.ShapeDtypeStruct((M, N), f32),
        compiler_params=pltpu.CompilerParams(dimension_semantics=("parallel",), vmem_limit_bytes=VMEM_LIMIT_BYTES),
        name=name,
    )(*operands)


def _diff_lambda(lam_ref):
    lq1, lk1, lq2, lk2 = (lam_ref[i:i + 1, :] for i in range(4))
    return (jnp.exp(jnp.sum(lq1 * lk1, axis=-1, keepdims=True))
            - jnp.exp(jnp.sum(lq2 * lk2, axis=-1, keepdims=True)) + LAM_INIT)


def _diff_finish(o0, o1, lam, g):
    o = o0 - lam * o1
    o = o * lax.rsqrt(jnp.mean(o * o, axis=-1, keepdims=True) + SUBLN_EPS) * g
    return o * (1.0 - LAM_INIT)


def _flash_prompt_kernel(qi_ref, kj_ref, q_ref, k_ref, v_ref, lam_ref, g_ref, o_ref, m_scr, l_scr, acc, bias_scr, *,
                         blk):
    h = pl.program_id(0)
    t = pl.program_id(1)
    i = qi_ref[t]
    j = kj_ref[t]
    slopes = [jnp.exp2(-(2 * h + m + 1).astype(f32)) * LOG2E for m in range(2)]

    @pl.when(j == 0)
    def _():
        m_scr[...] = jnp.full(m_scr.shape, -jnp.inf, f32)
        l_scr[...] = jnp.zeros(l_scr.shape, f32)
        acc[...] = jnp.zeros(acc.shape, f32)
        back = (lax.broadcasted_iota(jnp.int32, (blk, blk), 0)
                - lax.broadcasted_iota(jnp.int32, (blk, blk), 1)).astype(f32)
        for m in range(2):
            bias_scr[m, 0] = -slopes[m] * back
            bias_scr[m, 1] = jnp.where(back >= 0, -slopes[m] * back, -jnp.inf)

    diag = (j == i).astype(jnp.int32)
    v = v_ref[...].astype(bf16)
    for m in range(2):
        shift = -slopes[m] * ((i - j) * blk).astype(f32)
        q = (q_ref[:, m * DIFF_HD:(m + 1) * DIFF_HD] * (DIFF_HD ** -0.5 * LOG2E)).astype(bf16)
        k = k_ref[:, m * DIFF_HD:(m + 1) * DIFF_HD].astype(bf16)
        s = lax.dot_general(q, k, _NT, preferred_element_type=f32) + bias_scr[m, diag]
        m_old = m_scr[m]
        m_new = jnp.maximum(m_old, jnp.max(s, axis=-1, keepdims=True) + shift)
        p = jnp.exp2(s - (m_new[:, :1] - shift))
        alpha = jnp.exp2(m_old - m_new)
        l_scr[m] = alpha * l_scr[m] + jnp.sum(p, axis=-1, keepdims=True)
        acc[m] = alpha[:, :1] * acc[m] + lax.dot_general(p.astype(bf16), v, _NN, preferred_element_type=f32)
        m_scr[m] = m_new

    @pl.when(j == i)
    def _():
        o0 = acc[0] / l_scr[0][:, :1]
        o1 = acc[1] / l_scr[1][:, :1]
        o_ref[...] = _diff_finish(o0, o1, _diff_lambda(lam_ref), g_ref[...])


def _diff_attn_prompt(qkv, lam4, subln_g):
    T = qkv.shape[0]
    blk = FLASH_BLOCK
    nb = T // blk
    W = 2 * DIFF_HD
    pairs = [(i, j) for i in range(nb) for j in range(i + 1)]
    qi = jnp.asarray([p[0] for p in pairs], jnp.int32)
    kj = jnp.asarray([p[1] for p in pairs], jnp.int32)
    grid_spec = pltpu.PrefetchScalarGridSpec(
        num_scalar_prefetch=2,
        grid=(DIFF_HEADS, len(pairs)),
        in_specs=[pl.BlockSpec((blk, W), lambda h, t, qi, kj: (qi[t], h)),
                  pl.BlockSpec((blk, W), lambda h, t, qi, kj: (kj[t], DIFF_HEADS + h)),
                  pl.BlockSpec((blk, W), lambda h, t, qi, kj: (kj[t], 2 * DIFF_HEADS + h)),
                  pl.BlockSpec((4, DIFF_HD), lambda h, t, qi, kj: (0, 0)),
                  pl.BlockSpec((1, W), lambda h, t, qi, kj: (0, 0))],
        out_specs=pl.BlockSpec((blk, W), lambda h, t, qi, kj: (qi[t], h)),
        scratch_shapes=[pltpu.VMEM((2, blk, LANES), f32), pltpu.VMEM((2, blk, LANES), f32),
                        pltpu.VMEM((2, blk, W), f32), pltpu.VMEM((2, 2, blk, blk), f32)],
    )
    return pl.pallas_call(
        functools.partial(_flash_prompt_kernel, blk=blk),
        grid_spec=grid_spec,
        out_shape=jax.ShapeDtypeStruct((T, DIFF_HEADS * W), f32),
        compiler_params=pltpu.CompilerParams(dimension_semantics=("parallel", "arbitrary"),
                                             vmem_limit_bytes=VMEM_LIMIT_BYTES),
        name="diff_attn_prompt",
    )(qi, kj, qkv, qkv, qkv, lam4, subln_g.reshape(1, W))


def _paged_attn_kernel(pt_ref, q_ref, kn_ref, vn_ref, *refs, pages, page, n_new):
    k_refs, v_refs = refs[:pages], refs[pages:2 * pages]
    lam_ref, g_ref, o_ref, m_scr, l_scr, acc = refs[2 * pages:]
    j = pl.program_id(1)
    nj = pl.num_programs(1)
    rows = DIFF_HEADS * n_new
    span = pages * page
    past = nj * span

    @pl.when(j == 0)
    def _():
        m_scr[...] = jnp.full(m_scr.shape, -jnp.inf, f32)
        l_scr[...] = jnp.zeros(l_scr.shape, f32)
        acc[...] = jnp.zeros(acc.shape, f32)

    def update(w, s, v):
        m_old = m_scr[w]
        m_new = jnp.maximum(m_old, jnp.max(s, axis=-1, keepdims=True))
        p = jnp.exp(s - m_new[:, :1])
        alpha = jnp.exp(m_old - m_new)
        l_scr[w] = alpha * l_scr[w] + jnp.sum(p, axis=-1, keepdims=True)
        acc[w] = alpha[:, :1] * acc[w] + lax.dot_general(p.astype(bf16), v, _NN, preferred_element_type=f32)
        m_scr[w] = m_new

    def logits(w, k, n_keys, key_pos):
        row_head = lax.broadcasted_iota(jnp.int32, (rows, n_keys), 0) // n_new
        lane_head = lax.broadcasted_iota(jnp.int32, (rows, n_keys), 1) % DIFF_HEADS
        slope = jnp.exp2(-(2 * row_head + w + 1).astype(f32))
        q = (q_ref[w] * (DIFF_HD ** -0.5)).astype(bf16)
        s = lax.dot_general(q, k, _NT, preferred_element_type=f32) + slope * key_pos
        return jnp.where(row_head == lane_head, s, -jnp.inf)

    n_keys = span * DIFF_HEADS
    key_pos = ((lax.broadcasted_iota(jnp.int32, (1, n_keys), 1) // DIFF_HEADS) + (j * span - past)).astype(f32)
    v = jnp.concatenate([v_refs[p][...].reshape(page * DIFF_HEADS, 2 * DIFF_HD) for p in range(pages)],
                        axis=0).astype(bf16)
    for w in range(2):
        k = jnp.concatenate([k_refs[p][pl.ds(w, page * DIFF_HEADS, stride=2), :] for p in range(pages)],
                            axis=0).astype(bf16)
        update(w, logits(w, k, n_keys, key_pos), v)

    @pl.when(j == nj - 1)
    def _():
        new_keys = n_new * DIFF_HEADS
        key_tok = lax.broadcasted_iota(jnp.int32, (rows, new_keys), 1) // DIFF_HEADS
        causal = key_tok <= lax.broadcasted_iota(jnp.int32, (rows, new_keys), 0) % n_new
        vn = vn_ref[...].astype(bf16)
        for w in range(2):
            s = logits(w, kn_ref[w].astype(bf16), new_keys, key_tok[:1].astype(f32))
            update(w, jnp.where(causal, s, -jnp.inf), vn)
        o0 = acc[0] / l_scr[0][:, :1]
        o1 = acc[1] / l_scr[1][:, :1]
        o_ref[...] = _diff_finish(o0, o1, _diff_lambda(lam_ref), g_ref[...])


def _diff_attn_sample(q, k_new, v_new, cache_k, cache_v, page_table, lam4, subln_g):
    B, Ts = q.shape[:2]
    n_pages = page_table.shape[1]
    page = cache_k.shape[1]
    pages = SAMPLE_PAGES_PER_STEP
    rows = DIFF_HEADS * Ts
    by_parity = lambda x, order: jnp.transpose(x.reshape(B, Ts, DIFF_HEADS, 2, DIFF_HD), order).reshape(B, 2, rows, DIFF_HD)
    qw = by_parity(q, (0, 3, 2, 1, 4))
    kw = by_parity(k_new, (0, 3, 1, 2, 4))
    vn = v_new.reshape(B, rows, 2 * DIFF_HD)
    ck = cache_k.reshape(cache_k.shape[0], page * DIFF_MAPS, DIFF_HD)
    par = pl.BlockSpec((None, 2, rows, DIFF_HD), lambda b, j, pt: (b, 0, 0, 0))
    kspec = lambda p: pl.BlockSpec((None, page * DIFF_MAPS, DIFF_HD), lambda b, j, pt: (pt[b, j * pages + p], 0, 0))
    vspec = lambda p: pl.BlockSpec((None, page, DIFF_HEADS, 2 * DIFF_HD),
                                   lambda b, j, pt: (pt[b, j * pages + p], 0, 0, 0))
    grid_spec = pltpu.PrefetchScalarGridSpec(
        num_scalar_prefetch=1,
        grid=(B, n_pages // pages),
        in_specs=[par, par, pl.BlockSpec((None, rows, 2 * DIFF_HD), lambda b, j, pt: (b, 0, 0))]
                 + [kspec(p) for p in range(pages)] + [vspec(p) for p in range(pages)]
                 + [pl.BlockSpec((4, DIFF_HD), lambda b, j, pt: (0, 0)),
                    pl.BlockSpec((1, 2 * DIFF_HD), lambda b, j, pt: (0, 0))],
        out_specs=pl.BlockSpec((None, rows, 2 * DIFF_HD), lambda b, j, pt: (b, 0, 0)),
        scratch_shapes=[pltpu.VMEM((2, rows, LANES), f32), pltpu.VMEM((2, rows, LANES), f32),
                        pltpu.VMEM((2, rows, 2 * DIFF_HD), f32)],
    )
    o = pl.pallas_call(
        functools.partial(_paged_attn_kernel, pages=pages, page=page, n_new=Ts),
        grid_spec=grid_spec,
        out_shape=jax.ShapeDtypeStruct((B, rows, 2 * DIFF_HD), f32),
        compiler_params=pltpu.CompilerParams(dimension_semantics=("parallel", "arbitrary"),
                                             vmem_limit_bytes=VMEM_LIMIT_BYTES),
        name="diff_attn_sample",
    )(page_table, qw, kw, vn, *([ck] * pages), *([cache_v] * pages), lam4, subln_g.reshape(1, -1))
    return jnp.transpose(o.reshape(B, DIFF_HEADS, Ts, 2 * DIFF_HD), (0, 2, 1, 3))


def _split(x):
    hi = x.astype(bf16)
    lo = (x - hi.astype(f32)).astype(bf16)
    return hi, lo


def _dot3(a, b, dims=_NN):
    dg = lambda x, y: lax.dot_general(x, y, dims, preferred_element_type=f32)
    return dg(a[0], b[0]) + dg(a[0], b[1]) + dg(a[1], b[0])


def _split3_dot(x, ones_b, ones_first):
    hi = x.astype(bf16)
    r1 = x - hi.astype(f32)
    mid = r1.astype(bf16)
    lo = (r1 - mid.astype(f32)).astype(bf16)
    if ones_first:
        dn = lambda y: lax.dot_general(ones_b, y, _NN, preferred_element_type=f32)
    else:
        dn = lambda y: lax.dot_general(y, ones_b, _NN, preferred_element_type=f32)
    return dn(hi) + dn(mid) + dn(lo)


def _head_sums(x):
    same_head = (lax.broadcasted_iota(jnp.int32, (HEAD_PAIR_LANES, HEAD_PAIR_LANES), 0) // RWKV_HD
                 == lax.broadcasted_iota(jnp.int32, (HEAD_PAIR_LANES, HEAD_PAIR_LANES), 1) // RWKV_HD).astype(bf16)
    return jnp.concatenate(
        [_split3_dot(x[:, p * HEAD_PAIR_LANES:(p + 1) * HEAD_PAIR_LANES], same_head, False)
         for p in range(x.shape[1] // HEAD_PAIR_LANES)], axis=1)


def _rwkv_pre_kernel(p_ref, sh_ref, mu_ref, w0_ref, a0_ref, kk_ref, ka_ref, ww_ref, wa_ref, wg_ref,
                     r_ref, lw_ref, k_ref, v_ref, kkn_ref, a_ref, g_ref, carry):
    t = pl.program_id(1)

    @pl.when(t == 0)
    def _():
        carry[...] = sh_ref[...]

    p = p_ref[...]
    rows = p.shape[0]
    prev = jnp.concatenate([carry[...], p[:rows - 1]], axis=0)
    carry[...] = p[rows - 1:rows]
    xs = p + (prev - p) * mu_ref[...]
    k = xs[:, RWKV_W:2 * RWKV_W]
    lora = xs[:, LORA_LO:LORA_LO + LORA_WIN]
    gate = xs[:, GATE_LO:]
    dn = lambda x, w_ref: lax.dot_general(x.astype(bf16), w_ref[...], _NN, preferred_element_type=f32)
    z = -(w0_ref[...] + dn(jnp.tanh(lora), ww_ref))
    w_log = -(jnp.maximum(z, 0.0) + jnp.log(1.0 + jnp.exp(-jnp.abs(z)))) - 0.5
    a = 1.0 / (1.0 + jnp.exp(-(a0_ref[...] + dn(lora, wa_ref))))
    kk = k * kk_ref[...]
    norm = jnp.maximum(jnp.sqrt(_head_sums(kk * kk)), 1e-12)
    r_ref[...] = xs[:, :RWKV_W]
    lw_ref[...] = -jnp.exp(w_log)
    k_ref[...] = k * (1.0 + (a - 1.0) * ka_ref[...])
    v_ref[...] = xs[:, 2 * RWKV_W:3 * RWKV_W]
    kkn_ref[...] = kk / norm
    a_ref[...] = a
    g_ref[...] = dn(1.0 / (1.0 + jnp.exp(-gate)), wg_ref)


def _rwkv_pre(p, shift_prev, lp):
    B, T, _ = p.shape
    bt = min(T, RWKV_ROWS)
    pad_rows = lambda w, lo, n: jnp.zeros((n, RWKV_W), f32).at[lo:lo + w.shape[0]].set(w).astype(bf16)
    ww = pad_rows(lp['w_w2'], 0, LORA_WIN)
    wa = pad_rows(lp['w_a2'], DECAY_LORA, LORA_WIN)
    wg = pad_rows(lp['w_g2'], GATE_WIN - GATE_LORA, GATE_WIN)
    vec = lambda x: x.reshape(1, -1)
    const = lambda shape: pl.BlockSpec(shape, lambda b, t: (0,) * len(shape))
    seq = pl.BlockSpec((None, bt, RWKV_W), lambda b, t: (b, t, 0))
    return pl.pallas_call(
        _rwkv_pre_kernel,
        grid=(B, T // bt),
        in_specs=[pl.BlockSpec((None, bt, RWKV_COLS), lambda b, t: (b, t, 0)),
                  pl.BlockSpec((None, 1, RWKV_COLS), lambda b, t: (b, 0, 0)),
                  const((1, RWKV_COLS))] + [const((1, RWKV_W))] * 4
                 + [const((LORA_WIN, RWKV_W)), const((LORA_WIN, RWKV_W)), const((GATE_WIN, RWKV_W))],
        out_specs=[seq] * 7,
        out_shape=[jax.ShapeDtypeStruct((B, T, RWKV_W), f32)] * 7,
        scratch_shapes=[pltpu.VMEM((1, RWKV_COLS), f32)],
        compiler_params=pltpu.CompilerParams(dimension_semantics=("parallel", "arbitrary"),
                                             vmem_limit_bytes=VMEM_LIMIT_BYTES),
        name="rwkv_pre",
    )(p, shift_prev.reshape(B, 1, RWKV_COLS), vec(lp['shift_mu']), vec(lp['w0']), vec(lp['a0']), vec(lp['k_k']),
      vec(lp['k_a']), ww, wa, wg)


def _rwkv_post_kernel(y_ref, r_ref, k_ref, v_ref, g_ref, lw_ref, lb_ref, rk_ref, o_ref):
    y = y_ref[...]
    inv = 1.0 / RWKV_HD
    d = y - _head_sums(y) * inv
    var = _head_sums(d * d) * inv
    yn = d * lax.rsqrt(var + GN_EPS) * lw_ref[...] + lb_ref[...]
    bonus = _head_sums(r_ref[...] * k_ref[...] * rk_ref[...]) * v_ref[...]
    o_ref[...] = (yn + bonus) * g_ref[...]


def _rwkv_post(y, r, k, v, g, lp):
    B, T, W = y.shape
    bt = min(T, RWKV_ROWS)
    seq = pl.BlockSpec((None, bt, W), lambda b, t: (b, t, 0))
    const = pl.BlockSpec((1, W), lambda b, t: (0, 0))
    return pl.pallas_call(
        _rwkv_post_kernel,
        grid=(B, T // bt),
        in_specs=[seq] * 5 + [const] * 3,
        out_specs=seq,
        out_shape=jax.ShapeDtypeStruct((B, T, W), f32),
        compiler_params=pltpu.CompilerParams(dimension_semantics=("parallel", "parallel")),
        name="rwkv_post",
    )(y, r, k, v, g, lp['lnx_w'].reshape(1, W), lp['lnx_b'].reshape(1, W), lp['r_k'].reshape(1, W))


def _rwkv_scan_kernel(r_ref, lw_ref, k_ref, v_ref, kk_ref, a_ref, s0_ref, y_ref, sT_ref, state, *, pairs):
    c = pl.program_id(1)

    @pl.when(c == 0)
    def _():
        state[...] = s0_ref[...]

    C = r_ref.shape[0]
    C2 = 2 * C
    row = lax.broadcasted_iota(jnp.int32, (C2, C2), 0)
    col = lax.broadcasted_iota(jnp.int32, (C2, C2), 1)
    tril_strict = col < row
    tril_incl = col <= row
    eye = row == col
    eye_keys = (lax.broadcasted_iota(jnp.int32, (HEAD_PAIR_LANES, HEAD_PAIR_LANES), 0)
                == lax.broadcasted_iota(jnp.int32, (HEAD_PAIR_LANES, HEAD_PAIR_LANES), 1))
    first_head = lax.broadcasted_iota(jnp.int32, (C, HEAD_PAIR_LANES), 1) < RWKV_HD
    cum = (lax.broadcasted_iota(jnp.int32, (C, C), 1) <= lax.broadcasted_iota(jnp.int32, (C, C), 0)).astype(bf16)

    def stack(x):
        return jnp.concatenate([jnp.where(first_head, x, 0.0), jnp.where(first_head, 0.0, x)], axis=0)

    P = range(pairs)
    sls = [slice(p * HEAD_PAIR_LANES, (p + 1) * HEAD_PAIR_LANES) for p in P]
    each = lambda fn, *ls: [fn(*xs) for xs in zip(*ls)]

    lw = [lw_ref[:, sl] for sl in sls]
    cs = each(lambda x: _split3_dot(x, cum, True), lw)
    cl = [x[C - 1:C, :] for x in cs]
    kk = [kk_ref[:, sl] for sl in sls]
    kv = [k_ref[:, sl] for sl in sls]
    b = each(lambda x, sl: x * a_ref[:, sl], kk, sls)
    e_neg = each(lambda x: jnp.exp(-x), cs)
    e_end = each(lambda x, y: jnp.exp(x - y), cl, cs)
    A_s = each(lambda x, c_, l_: stack(-x * jnp.exp(c_ - l_)), kk, cs, lw)
    R_s = each(lambda sl, c_: stack(r_ref[:, sl] * jnp.exp(c_)), sls, cs)
    sB = each(lambda x, e: _split(stack(x * e)), b, e_neg)
    sK = each(lambda x, e: _split(stack(x * e)), kv, e_neg)
    sV = each(lambda sl: _split(stack(v_ref[:, sl])), sls)
    sA = each(_split, A_s)
    sR = each(_split, R_s)
    BhT = each(lambda x, e: stack(x * e).T, b, e_end)
    KhT = each(lambda x, e: stack(x * e).T, kv, e_end)
    L = each(lambda x, y: jnp.where(tril_strict, _dot3(x, y, _NT), 0.0), sA, sB)
    Mak = each(lambda x, y: jnp.where(tril_strict, _dot3(x, y, _NT), 0.0), sA, sK)
    Mrb = each(lambda x, y: jnp.where(tril_incl, _dot3(x, y, _NT), 0.0), sR, sB)
    Mrk = each(lambda x, y: jnp.where(tril_incl, _dot3(x, y, _NT), 0.0), sR, sK)
    T = each(lambda x: jnp.where(eye, 1.0, 0.0).astype(f32) + x, L)
    Lp = L
    n = 2
    while n < C:
        sLp = each(_split, Lp)
        Lp = each(lambda x: _dot3(x, x), sLp)
        T = each(lambda t, lp_: t + _dot3(_split(t), _split(lp_)), T, Lp)
        n *= 2
    X1 = each(lambda m, v: _dot3(_split(m), v), Mak, sV)
    sPU = each(lambda t, a_, x: _split(_dot3(_split(t), _split(jnp.concatenate([a_, x], axis=1)))), T, A_s, X1)
    QY = each(lambda m, pu: _dot3(_split(m), pu), Mrb, sPU)
    Q = each(lambda r_, qy: r_ + qy[:, :HEAD_PAIR_LANES], R_s, QY)
    Yl = each(lambda qy, m, v: qy[:, HEAD_PAIR_LANES:] + _dot3(_split(m), v), QY, Mrk, sV)
    GH = each(lambda bt_, pu: _dot3(_split(bt_), pu), BhT, sPU)
    G = each(lambda c_, gh: jnp.where(eye_keys, jnp.exp(c_), 0.0) + gh[:, :HEAD_PAIR_LANES], cl, GH)
    H = each(lambda gh, kt, v: gh[:, HEAD_PAIR_LANES:] + _dot3(_split(kt), v), GH, KhT, sV)
    sS = [_split(state[p]) for p in P]
    Y = each(lambda q, s, yl: _dot3(_split(q), s) + yl, Q, sS, Yl)
    S_new = each(lambda g, s, h: _dot3(_split(g), s) + h, G, sS, H)
    for p in P:
        state[p] = S_new[p]
        y_ref[:, sls[p]] = Y[p][:C] + Y[p][C:]

    @pl.when(c == pl.num_programs(1) - 1)
    def _():
        sT_ref[...] = state[...]


def _rwkv_scan(r, lw, k, v, kk, a, s0, chunk):
    B, T, W = r.shape
    pairs = W // HEAD_PAIR_LANES
    seq = pl.BlockSpec((None, chunk, W), lambda b_, c: (b_, c, 0))
    st = pl.BlockSpec((None, pairs, HEAD_PAIR_LANES, HEAD_PAIR_LANES), lambda b_, c: (b_, 0, 0, 0))
    return pl.pallas_call(
        functools.partial(_rwkv_scan_kernel, pairs=pairs),
        grid=(B, T // chunk),
        in_specs=[seq] * 6 + [st],
        out_specs=[seq, st],
        out_shape=[jax.ShapeDtypeStruct((B, T, W), f32), jax.ShapeDtypeStruct(s0.shape, f32)],
        scratch_shapes=[pltpu.VMEM((pairs, HEAD_PAIR_LANES, HEAD_PAIR_LANES), f32)],
        compiler_params=pltpu.CompilerParams(dimension_semantics=("parallel", "arbitrary")),
        name="rwkv_scan",
    )(r, lw, k, v, kk, a, s0)


def _state_to_stacked(S):
    B, H, N, _ = S.shape
    St = jnp.swapaxes(S, -1, -2).reshape(B, H // 2, 2, N, N)
    z = jnp.zeros_like(St[:, :, 0])
    top = jnp.concatenate([St[:, :, 0], z], axis=-1)
    bot = jnp.concatenate([z, St[:, :, 1]], axis=-1)
    return jnp.concatenate([top, bot], axis=-2)


def _stacked_to_state(Ss):
    B, P, _, _ = Ss.shape
    N = RWKV_HD
    St = jnp.stack([Ss[:, :, :N, :N], Ss[:, :, N:, N:]], axis=2).reshape(B, 2 * P, N, N)
    return jnp.swapaxes(St, -1, -2)


def _rwkv7_mix(p, shift_prev, wkv0, lp):
    B, T, _ = p.shape
    r, lw, k, v, kk, a, g = _rwkv_pre(p, shift_prev, lp)
    chunk = min(RWKV_CHUNK, -(-T // SUBLANES) * SUBLANES)
    Tp = -(-T // chunk) * chunk
    pad = lambda x: jnp.pad(x, ((0, 0), (0, Tp - T), (0, 0)))
    y, s_end = _rwkv_scan(pad(r), pad(lw), pad(k), pad(v), pad(kk), pad(a), _state_to_stacked(wkv0), chunk)
    out = _rwkv_post(y[:, :T], r, k, v, g, lp)
    return out, p[:, -1], _stacked_to_state(s_end)


def _softmax_rows(s):
    e = jnp.exp(s - jnp.max(s, axis=-1, keepdims=True))
    return e / jnp.sum(e, axis=-1, keepdims=True)


def _cross_prompt_kernel(q_ref, k_ref, v_ref, o_ref):
    for h in range(MEM_HEADS):
        sl = slice(h * MEM_HD, (h + 1) * MEM_HD)
        s = lax.dot_general(q_ref[:, sl].astype(bf16), k_ref[:, sl].astype(bf16), _NT, preferred_element_type=f32)
        p = _softmax_rows(s * (MEM_HD ** -0.5))
        o_ref[:, sl] = lax.dot_general(p.astype(bf16), v_ref[:, sl].astype(bf16), _NN, preferred_element_type=f32)


def _cross_attn_prompt(q, mk, mv):
    T, D = q.shape
    M = mk.shape[0]
    bt = min(T, CROSS_ROWS)
    mem = pl.BlockSpec((M, D), lambda i: (0, 0))
    return pl.pallas_call(
        _cross_prompt_kernel,
        grid=(T // bt,),
        in_specs=[pl.BlockSpec((bt, D), lambda i: (i, 0)), mem, mem],
        out_specs=pl.BlockSpec((bt, D), lambda i: (i, 0)),
        out_shape=jax.ShapeDtypeStruct((T, D), f32),
        compiler_params=pltpu.CompilerParams(dimension_semantics=("parallel",), vmem_limit_bytes=VMEM_LIMIT_BYTES),
        name="cross_attn_prompt",
    )(q, mk, mv)


def _cross_sample_kernel(q_ref, k_ref, v_ref, o_ref):
    ts = q_ref.shape[0]
    n = k_ref.shape[0] * MEM_HEADS
    rows = MEM_HEADS * ts
    q = jnp.concatenate([q_ref[:, h * MEM_HD:(h + 1) * MEM_HD] for h in range(MEM_HEADS)], axis=0).astype(bf16)
    k = k_ref[...].reshape(n, MEM_HD).astype(bf16)
    v = v_ref[...].reshape(n, MEM_HD).astype(bf16)
    s = lax.dot_general(q, k, _NT, preferred_element_type=f32) * (MEM_HD ** -0.5)
    own = ((lax.broadcasted_iota(jnp.int32, (rows, n), 0) // ts)
           == (lax.broadcasted_iota(jnp.int32, (rows, n), 1) % MEM_HEADS))
    p = _softmax_rows(jnp.where(own, s, -jnp.inf))
    o = lax.dot_general(p.astype(bf16), v, _NN, preferred_element_type=f32)
    for h in range(MEM_HEADS):
        o_ref[:, h * MEM_HD:(h + 1) * MEM_HD] = o[h * ts:(h + 1) * ts]


def _cross_attn_sample(q, mk, mv):
    B, Ts, D = q.shape
    M = mk.shape[1]
    mem = pl.BlockSpec((None, M, MEM_HEADS, MEM_HD), lambda b: (b, 0, 0, 0))
    tok = pl.BlockSpec((None, Ts, D), lambda b: (b, 0, 0))
    return pl.pallas_call(
        _cross_sample_kernel,
        grid=(B,),
        in_specs=[tok, mem, mem],
        out_specs=tok,
        out_shape=jax.ShapeDtypeStruct((B, Ts, D), f32),
        compiler_params=pltpu.CompilerParams(dimension_semantics=("parallel",), vmem_limit_bytes=VMEM_LIMIT_BYTES),
        name="cross_attn_sample",
    )(q, mk, mv)


def _peer_route_kernel(h_ref, g_ref, wq_ref, keys_ref, xn_ref, r2_ref, e2_ref, e1_ref, wd_ref,
                       q_scr, top_scr, cand_scr):
    x = h_ref[...]
    xn = x * lax.rsqrt(jnp.mean(x * x, axis=-1, keepdims=True) + NORM_EPS) * g_ref[...]
    xb = xn.astype(bf16)
    xn_ref[...] = xb
    q_scr[...] = lax.dot_general(xb, wq_ref[...], _NN, preferred_element_type=f32)
    neg = -jnp.inf
    bt = x.shape[0]
    key_id = lax.broadcasted_iota(jnp.int32, (N_KEYS, bt), 0)
    cand_id = lax.broadcasted_iota(jnp.int32, (CAND_ROWS, bt), 0)

    def pop_max(x_, ids):
        m = jnp.max(x_, axis=0, keepdims=True)
        hit = ids == jnp.min(jnp.where(x_ == m, ids, ids.shape[0]), axis=0, keepdims=True)
        return m, hit, jnp.where(hit, neg, x_)

    def per_head(h, carry):
        ranks = []
        scores = []
        for c in range(2):
            off = pl.multiple_of((h * 2 + c) * (PEER_DK // 2), LANES)
            qhc = q_scr[:, pl.ds(off, PEER_DK // 2)].astype(bf16)
            s = lax.dot_general(keys_ref[h, c], qhc, _NT, preferred_element_type=f32)
            scores.append(s)
            x_ = s
            rank = jnp.full(s.shape, PEER_TOPK, f32)
            for i in range(PEER_TOPK):
                m, hit, x_ = pop_max(x_, key_id)
                top_scr[c, i:i + 1, :] = m
                rank = jnp.where(hit, jnp.float32(i), rank)
            ranks.append(rank)
        s1, s2 = scores
        for i in range(PEER_TOPK):
            n_i = PEER_TOPK // (i + 1)
            cand_scr[CAND_OFFSETS[i]:CAND_OFFSETS[i] + n_i, :] = top_scr[0, i:i + 1, :] + top_scr[1, 0:n_i, :]
        cand_scr[CAND_OFFSETS[PEER_TOPK]:, :] = jnp.full((CAND_ROWS - CAND_OFFSETS[PEER_TOPK], bt), neg, f32)
        best = top_scr[0, 0:1, :] + top_scr[1, 0:1, :]
        x_ = cand_scr[...]
        z = jnp.zeros_like(best)
        for i in range(PEER_TOPK):
            m, _, x_ = pop_max(x_, cand_id)
            z = z + jnp.exp(m - best)
        taken = (x_ == neg).astype(f32)
        width = jnp.zeros(s1.shape, f32)
        for i in range(PEER_TOPK):
            n_i = PEER_TOPK // (i + 1)
            t_i = jnp.sum(taken[CAND_OFFSETS[i]:CAND_OFFSETS[i] + n_i], axis=0, keepdims=True)
            width = jnp.where(ranks[0] == jnp.float32(i), t_i, width)
        r2_ref[h] = ranks[1]
        e2_ref[h] = jnp.exp(s2 - top_scr[1, 0:1, :])
        e1_ref[h] = jnp.exp(s1 - top_scr[0, 0:1, :]) / z
        wd_ref[h] = width
        return carry

    lax.fori_loop(0, PEER_HEADS, per_head, 0)


def _peer_route(h, g, wq_b, keys_b):
    T, D = h.shape
    bt = _peer_token_block(T)
    tab = jax.ShapeDtypeStruct((PEER_HEADS, N_KEYS, T), f32)
    tab_spec = pl.BlockSpec((PEER_HEADS, N_KEYS, bt), lambda i: (0, 0, i))
    return pl.pallas_call(
        _peer_route_kernel,
        grid=(T // bt,),
        in_specs=[pl.BlockSpec((bt, D), lambda i: (i, 0)), pl.BlockSpec((1, D), lambda i: (0, 0)),
                  pl.BlockSpec((D, PEER_HEADS * PEER_DK), lambda i: (0, 0), pipeline_mode=pl.Buffered(1)),
                  pl.BlockSpec((PEER_HEADS, 2, N_KEYS, PEER_DK // 2), lambda i: (0, 0, 0, 0))],
        out_specs=[pl.BlockSpec((bt, D), lambda i: (i, 0))] + [tab_spec] * 4,
        out_shape=[jax.ShapeDtypeStruct((T, D), bf16)] + [tab] * 4,
        scratch_shapes=[pltpu.VMEM((bt, PEER_HEADS * PEER_DK), f32), pltpu.VMEM((2, PEER_TOPK, bt), f32),
                        pltpu.VMEM((CAND_ROWS, bt), f32)],
        compiler_params=pltpu.CompilerParams(dimension_semantics=("parallel",), vmem_limit_bytes=VMEM_LIMIT_BYTES),
        name="peer_route",
    )(h, g.reshape(1, D), wq_b, keys_b)


def _peer_mix_kernel(xn_ref, u_ref, v_ref, r2_ref, e2_ref, e1_ref, wd_ref, h_ref, g_ref, y_ref):
    j = pl.program_id(1)

    @pl.when(j == 0)
    def _():
        y_ref[...] = h_ref[...]

    per = PEER_EXPERT_CHUNK // PEER_MIX_SLICES
    xn = xn_ref[...]
    hts = [lax.dot_general(u_ref[k * per:(k + 1) * per, :], xn, _NT, preferred_element_type=f32)
           for k in range(PEER_MIX_SLICES)]
    total = None
    for k in range(PEER_MIX_SLICES):
        ht = hts[k]
        hid = 0.5 * ht * (1.0 + lax.erf(ht * (1.0 / math.sqrt(2.0))))
        rows = []
        for a in range(k * per // N_KEYS, (k + 1) * per // N_KEYS):
            g = None
            for h in range(PEER_HEADS):
                t = jnp.where(r2_ref[h] < wd_ref[h, a:a + 1, :], e2_ref[h] * e1_ref[h, a:a + 1, :], 0.0)
                g = t if g is None else g + t
            rows.append(g)
        gate = jnp.concatenate(rows, axis=0)
        w = (gate * hid).T.astype(bf16)
        d = lax.dot_general(w, v_ref[k * per:(k + 1) * per, :], _NN, preferred_element_type=f32)
        total = d if total is None else total + d
    y_ref[...] += total

    @pl.when(j == pl.num_programs(1) - 1)
    def _():
        x = y_ref[...]
        y_ref[...] = x * lax.rsqrt(jnp.mean(x * x, axis=-1, keepdims=True) + NORM_EPS) * g_ref[...]


def _peer_mix(xn_b, u_b, v_b, r2, e2, e1, wd, h, final_g):
    T, D = h.shape
    E = u_b.shape[0]
    bt = _peer_token_block(T)
    EC = PEER_EXPERT_CHUNK
    tok1 = pl.BlockSpec((bt, D), lambda i, j: (i, 0), pipeline_mode=pl.Buffered(1))
    wts = pl.BlockSpec((EC, D), lambda i, j: (j, 0))
    full = pl.BlockSpec((PEER_HEADS, N_KEYS, bt), lambda i, j: (0, 0, i), pipeline_mode=pl.Buffered(1))
    part = pl.BlockSpec((PEER_HEADS, EC // N_KEYS, bt), lambda i, j: (0, j, i))
    return pl.pallas_call(
        _peer_mix_kernel,
        grid=(T // bt, E // EC),
        in_specs=[tok1, wts, wts, full, full, part, part, tok1, pl.BlockSpec((1, D), lambda i, j: (0, 0))],
        out_specs=pl.BlockSpec((bt, D), lambda i, j: (i, 0)),
        out_shape=jax.ShapeDtypeStruct((T, D), f32),
        compiler_params=pltpu.CompilerParams(dimension_semantics=("parallel", "arbitrary"),
                                             vmem_limit_bytes=VMEM_LIMIT_BYTES),
        name="peer_mix",
    )(xn_b, u_b, v_b, r2, e2, e1, wd, h, final_g.reshape(1, D))


def kernel(x_prompt, x_sample, cache_k, cache_v, state_wkv, state_shift, cache_mem_k, cache_mem_v, page_table, mem_prompt, norm1_g, w_in, lam_q1, lam_k1, lam_q2, lam_k2, subln_g, shift_mu, w0, w_w2, a0, w_a2, w_g2, k_k, k_a, r_k, lnx_w, lnx_b, w_out, norm2_g, mem_norm_g, w_cq, w_mk, w_mv, w_co, norm3_g, w_pq, peer_keys, peer_u, peer_v, final_g):
    l = 0
    D = D_MODEL
    Bp, Tp, _ = x_prompt.shape
    Bs, Ts, _ = x_sample.shape
    assert Bp == 1, "the prompt group is handled as one sequence"
    lp = dict(shift_mu=shift_mu[l], w0=w0[l], w_w2=w_w2[l], a0=a0[l], w_a2=w_a2[l], w_g2=w_g2[l], k_k=k_k[l],
              k_a=k_a[l], r_k=r_k[l], lnx_w=lnx_w[l], lnx_b=lnx_b[l])
    w_qkv = w_in[l][:, :3 * DIFF_W].astype(bf16)
    w_rwkv = w_in[l][:, 3 * DIFF_W:].astype(bf16)
    w_out_b, w_cq_b, w_co_b = w_out[l].astype(bf16), w_cq[l].astype(bf16), w_co[l].astype(bf16)
    w_mk_b, w_mv_b, w_pq_b = w_mk[l].astype(bf16), w_mv[l].astype(bf16), w_pq[l].astype(bf16)
    keys_b, u_b, v_b = peer_keys[l].astype(bf16), peer_u[l].astype(bf16), peer_v[l].astype(bf16)
    lam4 = jnp.stack([lam_q1[l], lam_k1[l], lam_q2[l], lam_k2[l]])

    xp = x_prompt.reshape(Bp * Tp, D)
    xs = x_sample.reshape(Bs * Ts, D)

    mem = mem_prompt.reshape(-1, D)
    mk_p = _matmul(mem, w_mk_b, gain=mem_norm_g[l], name="mem_k")
    mv_p = _matmul(mem, w_mv_b, gain=mem_norm_g[l], name="mem_v")

    qkv_p = _matmul(xp, w_qkv, gain=norm1_g[l], name="in_proj_attn")
    rw_p = _matmul(xp, w_rwkv, gain=norm1_g[l], name="in_proj_rwkv")
    a_p = _diff_attn_prompt(qkv_p, lam4, subln_g[l])
    r_p, sh_p, wkv_p = _rwkv7_mix(rw_p.reshape(Bp, Tp, RWKV_COLS), jnp.zeros((Bp, RWKV_COLS), f32),
                                  jnp.zeros((Bp, RWKV_HEADS, RWKV_HD, RWKV_HD), f32), lp)
    h_p = _matmul(jnp.concatenate([a_p, r_p.reshape(Bp * Tp, RWKV_W)], axis=-1), w_out_b, residual=xp, name="out_proj")

    qkv_s = _matmul(xs, w_qkv, gain=norm1_g[l], name="in_proj_attn")
    rw_s = _matmul(xs, w_rwkv, gain=norm1_g[l], name="in_proj_rwkv")
    q_s = qkv_s[:, :DIFF_W].reshape(Bs, Ts, DIFF_MAPS, DIFF_HD)
    k_s = qkv_s[:, DIFF_W:2 * DIFF_W].reshape(Bs, Ts, DIFF_MAPS, DIFF_HD)
    v_s = qkv_s[:, 2 * DIFF_W:].reshape(Bs, Ts, DIFF_HEADS, 2 * DIFF_HD)
    a_s = _diff_attn_sample(q_s, k_s, v_s, cache_k[l], cache_v[l], page_table, lam4, subln_g[l])
    r_s, sh_s, wkv_s = _rwkv7_mix(rw_s.reshape(Bs, Ts, RWKV_COLS), state_shift[l], state_wkv[l], lp)
    h_s = _matmul(jnp.concatenate([a_s.reshape(Bs * Ts, DIFF_W), r_s.reshape(Bs * Ts, RWKV_W)], axis=-1), w_out_b,
                  residual=xs, name="out_proj")

    cq_p = _matmul(h_p, w_cq_b, gain=norm2_g[l], name="cross_q")
    cq_s = _matmul(h_s, w_cq_b, gain=norm2_g[l], name="cross_q")
    co_p = _cross_attn_prompt(cq_p, mk_p, mv_p)
    co_s = _cross_attn_sample(cq_s.reshape(Bs, Ts, D), cache_mem_k[l], cache_mem_v[l]).reshape(Bs * Ts, D)
    h_p = _matmul(co_p, w_co_b, residual=h_p, name="cross_out")
    h_s = _matmul(co_s, w_co_b, residual=h_s, name="cross_out")

    h_all = jnp.concatenate([h_p, h_s], axis=0)
    xn_b, r2, e2, e1, wd = _peer_route(h_all, norm3_g[l], w_pq_b, keys_b)
    y_all = _peer_mix(xn_b, u_b, v_b, r2, e2, e1, wd, h_all, final_g)
    y_prompt = y_all[:Bp * Tp].reshape(Bp, Tp, D)
    y_sample = y_all[Bp * Tp:].reshape(Bs, Ts, D)

    k_p = qkv_p[:, DIFF_W:2 * DIFF_W].reshape(Bp, Tp, DIFF_MAPS, DIFF_HD)
    v_p = qkv_p[:, 2 * DIFF_W:].reshape(Bp, Tp, DIFF_HEADS, 2 * DIFF_HD)
    mem_shape = (Bp, -1, MEM_HEADS, MEM_HD)
    return (y_prompt, y_sample, k_p[None], v_p[None], k_s[None], v_s[None],
            wkv_p[None], sh_p[None], wkv_s[None], sh_s[None], mk_p.reshape(mem_shape)[None], mv_p.reshape(mem_shape)[None])
```

```python
import functools
import math

import jax
import jax.numpy as jnp
from jax import lax
from jax.experimental import pallas as pl
from jax.experimental.pallas import tpu as pltpu

f32, bf16 = jnp.float32, jnp.bfloat16

D_MODEL = 2048
DIFF_HD = 128
DIFF_HEADS = 4
DIFF_MAPS = 8
DIFF_W = 1024
SUBLN_EPS = 1e-5
LAM_INIT = 0.8 - 0.6 * math.exp(-0.3 * 0)
RWKV_HD = 64
RWKV_W = 1024
RWKV_HEADS = 16
DECAY_LORA = 96
ICLR_LORA = 96
GATE_LORA = 256
RWKV_COLS = 3 * RWKV_W + DECAY_LORA + ICLR_LORA + GATE_LORA
GN_EPS = 64e-5
MEM_HEADS = 4
MEM_HD = D_MODEL // MEM_HEADS
PEER_HEADS = 8
N_KEYS = 128
PEER_TOPK = 16
PEER_DK = 256
NORM_EPS = 1e-6

VMEM_LIMIT_BYTES = 60 << 20
LANES = 128
SUBLANES = 8

RWKV_CHUNK = 64
HEAD_PAIR_LANES = 2 * RWKV_HD
RWKV_ROWS = 256
FLASH_BLOCK = 512
SAMPLE_PAGES_PER_STEP = 8
LOG2E = math.log2(math.e)
CROSS_ROWS = 512
PEER_TOKEN_BLOCK_MAX = 640
PEER_EXPERT_CHUNK = 1024
PEER_MIX_SLICES = 2
MATMUL_ROWS = 512
CAND_OFFSETS = [sum(PEER_TOPK // (k + 1) for k in range(i)) for i in range(PEER_TOPK + 1)]
CAND_ROWS = -(-CAND_OFFSETS[PEER_TOPK] // SUBLANES) * SUBLANES

LORA_LO = 3 * RWKV_W
LORA_WIN = 2 * LANES
GATE_LO = RWKV_COLS - GATE_LORA - RWKV_HD
GATE_WIN = RWKV_COLS - GATE_LO


def _peer_token_block(n_tokens):
    return max(b for b in range(LANES, PEER_TOKEN_BLOCK_MAX + 1, LANES) if n_tokens % b == 0)


_NN = (((1,), (0,)), ((), ()))
_NT = (((1,), (1,)), ((), ()))


def _matmul_kernel(*refs, norm, residual):
    x_ref, refs = refs[0], refs[1:]
    if norm:
        g_ref, refs = refs[0], refs[1:]
    w_ref, refs = refs[0], refs[1:]
    if residual:
        r_ref, refs = refs[0], refs[1:]
    o_ref = refs[0]
    x = x_ref[...]
    if norm:
        x = x * lax.rsqrt(jnp.mean(x * x, axis=-1, keepdims=True) + NORM_EPS) * g_ref[...]
    y = lax.dot_general(x.astype(bf16), w_ref[...], _NN, preferred_element_type=f32)
    if residual:
        y = r_ref[...] + y
    o_ref[...] = y


def _matmul(x, w_b, gain=None, residual=None, name="matmul"):
    M, K = x.shape
    N = w_b.shape[1]
    bm = min(MATMUL_ROWS, M)
    operands = [x]
    specs = [pl.BlockSpec((bm, K), lambda i: (i, 0))]
    if gain is not None:
        operands.append(gain.reshape(1, K))
        specs.append(pl.BlockSpec((1, K), lambda i: (0, 0)))
    operands.append(w_b)
    specs.append(pl.BlockSpec((K, N), lambda i: (0, 0), pipeline_mode=pl.Buffered(1)))
    if residual is not None:
        operands.append(residual)
        specs.append(pl.BlockSpec((bm, N), lambda i: (i, 0)))
    return pl.pallas_call(
        functools.partial(_matmul_kernel, norm=gain is not None, residual=residual is not None),
        grid=(M // bm,),
        in_specs=specs,
        out_specs=pl.BlockSpec((bm, N), lambda i: (i, 0)),
        out_shape=jax.ShapeDtypeStruct((M, N), f32),
        compiler_params=pltpu.CompilerParams(dimension_semantics=("parallel",), vmem_limit_bytes=VMEM_LIMIT_BYTES),
        name=name,
    )(*operands)


def _diff_lambda(lam_ref):
    lq1, lk1, lq2, lk2 = (lam_ref[i:i + 1, :] for i in range(4))
    return (jnp.exp(jnp.sum(lq1 * lk1, axis=-1, keepdims=True))
            - jnp.exp(jnp.sum(lq2 * lk2, axis=-1, keepdims=True)) + LAM_INIT)


def _diff_finish(o0, o1, lam, g):
    o = o0 - lam * o1
    o = o * lax.rsqrt(jnp.mean(o * o, axis=-1, keepdims=True) + SUBLN_EPS) * g
    return o * (1.0 - LAM_INIT)


def _flash_prompt_kernel(qi_ref, kj_ref, q_ref, k_ref, v_ref, lam_ref, g_ref, o_ref, m_scr, l_scr, acc, bias_scr, *,
                         blk):
    h = pl.program_id(0)
    t = pl.program_id(1)
    i = qi_ref[t]
    j = kj_ref[t]
    slopes = [jnp.exp2(-(2 * h + m + 1).astype(f32)) * LOG2E for m in range(2)]

    @pl.when(j == 0)
    def _():
        m_scr[...] = jnp.full(m_scr.shape, -jnp.inf, f32)
        l_scr[...] = jnp.zeros(l_scr.shape, f32)
        acc[...] = jnp.zeros(acc.shape, f32)
        back = (lax.broadcasted_iota(jnp.int32, (blk, blk), 0)
                - lax.broadcasted_iota(jnp.int32, (blk, blk), 1)).astype(f32)
        for m in range(2):
            bias_scr[m, 0] = -slopes[m] * back
            bias_scr[m, 1] = jnp.where(back >= 0, -slopes[m] * back, -jnp.inf)

    diag = (j == i).astype(jnp.int32)
    v = v_ref[...].astype(bf16)
    for m in range(2):
        shift = -slopes[m] * ((i - j) * blk).astype(f32)
        q = (q_ref[:, m * DIFF_HD:(m + 1) * DIFF_HD] * (DIFF_HD ** -0.5 * LOG2E)).astype(bf16)
        k = k_ref[:, m * DIFF_HD:(m + 1) * DIFF_HD].astype(bf16)
        s = lax.dot_general(q, k, _NT, preferred_element_type=f32) + bias_scr[m, diag]
        cols = [s[:, c * LANES:(c + 1) * LANES] for c in range(blk // LANES)]
        m_old = m_scr[m]
        m_new = jnp.maximum(m_old, jnp.max(functools.reduce(jnp.maximum, cols), axis=-1, keepdims=True) + shift)
        level = m_new - shift
        ps = [jnp.exp2(c - level) for c in cols]
        alpha = jnp.exp2(m_old - m_new)
        l_scr[m] = alpha * l_scr[m] + jnp.sum(functools.reduce(jnp.add, ps), axis=-1, keepdims=True)
        pv = lax.dot_general(jnp.concatenate(ps, axis=1).astype(bf16), v, _NN, preferred_element_type=f32)
        acc[m] = jnp.concatenate([alpha, alpha], axis=1) * acc[m] + pv
        m_scr[m] = m_new

    @pl.when(j == i)
    def _():
        inv = [1.0 / l_scr[m] for m in range(2)]
        o0 = acc[0] * jnp.concatenate([inv[0], inv[0]], axis=1)
        o1 = acc[1] * jnp.concatenate([inv[1], inv[1]], axis=1)
        o_ref[...] = _diff_finish(o0, o1, _diff_lambda(lam_ref), g_ref[...])


def _diff_attn_prompt(qkv, lam4, subln_g):
    T = qkv.shape[0]
    blk = FLASH_BLOCK
    nb = T // blk
    W = 2 * DIFF_HD
    pairs = [(i, j) for i in range(nb) for j in range(i + 1)]
    qi = jnp.asarray([p[0] for p in pairs], jnp.int32)
    kj = jnp.asarray([p[1] for p in pairs], jnp.int32)
    grid_spec = pltpu.PrefetchScalarGridSpec(
        num_scalar_prefetch=2,
        grid=(DIFF_HEADS, len(pairs)),
        in_specs=[pl.BlockSpec((blk, W), lambda h, t, qi, kj: (qi[t], h)),
                  pl.BlockSpec((blk, W), lambda h, t, qi, kj: (kj[t], DIFF_HEADS + h)),
                  pl.BlockSpec((blk, W), lambda h, t, qi, kj: (kj[t], 2 * DIFF_HEADS + h)),
                  pl.BlockSpec((4, DIFF_HD), lambda h, t, qi, kj: (0, 0)),
                  pl.BlockSpec((1, W), lambda h, t, qi, kj: (0, 0))],
        out_specs=pl.BlockSpec((blk, W), lambda h, t, qi, kj: (qi[t], h)),
        scratch_shapes=[pltpu.VMEM((2, blk, LANES), f32), pltpu.VMEM((2, blk, LANES), f32),
                        pltpu.VMEM((2, blk, W), f32), pltpu.VMEM((2, 2, blk, blk), f32)],
    )
    return pl.pallas_call(
        functools.partial(_flash_prompt_kernel, blk=blk),
        grid_spec=grid_spec,
        out_shape=jax.ShapeDtypeStruct((T, DIFF_HEADS * W), f32),
        compiler_params=pltpu.CompilerParams(dimension_semantics=("parallel", "arbitrary"),
                                             vmem_limit_bytes=VMEM_LIMIT_BYTES),
        name="diff_attn_prompt",
    )(qi, kj, qkv, qkv, qkv, lam4, subln_g.reshape(1, W))


def _paged_attn_kernel(pt_ref, q_ref, kn_ref, vn_ref, *refs, pages, page, n_new):
    k_refs, v_refs = refs[:pages], refs[pages:2 * pages]
    lam_ref, g_ref, o_ref, m_scr, l_scr, acc = refs[2 * pages:]
    j = pl.program_id(1)
    nj = pl.num_programs(1)
    rows = DIFF_HEADS * n_new
    span = pages * page
    past = nj * span

    @pl.when(j == 0)
    def _():
        m_scr[...] = jnp.full(m_scr.shape, -jnp.inf, f32)
        l_scr[...] = jnp.zeros(l_scr.shape, f32)
        acc[...] = jnp.zeros(acc.shape, f32)

    def update(w, s, v):
        m_old = m_scr[w]
        m_new = jnp.maximum(m_old, jnp.max(s, axis=-1, keepdims=True))
        p = jnp.exp(s - m_new[:, :1])
        alpha = jnp.exp(m_old - m_new)
        l_scr[w] = alpha * l_scr[w] + jnp.sum(p, axis=-1, keepdims=True)
        acc[w] = alpha[:, :1] * acc[w] + lax.dot_general(p.astype(bf16), v, _NN, preferred_element_type=f32)
        m_scr[w] = m_new

    def logits(w, k, n_keys, key_pos):
        row_head = lax.broadcasted_iota(jnp.int32, (rows, n_keys), 0) // n_new
        lane_head = lax.broadcasted_iota(jnp.int32, (rows, n_keys), 1) % DIFF_HEADS
        slope = jnp.exp2(-(2 * row_head + w + 1).astype(f32))
        q = (q_ref[w] * (DIFF_HD ** -0.5)).astype(bf16)
        s = lax.dot_general(q, k, _NT, preferred_element_type=f32) + slope * key_pos
        return jnp.where(row_head == lane_head, s, -jnp.inf)

    n_keys = span * DIFF_HEADS
    key_pos = ((lax.broadcasted_iota(jnp.int32, (1, n_keys), 1) // DIFF_HEADS) + (j * span - past)).astype(f32)
    v = jnp.concatenate([v_refs[p][...].reshape(page * DIFF_HEADS, 2 * DIFF_HD) for p in range(pages)],
                        axis=0).astype(bf16)
    for w in range(2):
        k = jnp.concatenate([k_refs[p][pl.ds(w, page * DIFF_HEADS, stride=2), :] for p in range(pages)],
                            axis=0).astype(bf16)
        update(w, logits(w, k, n_keys, key_pos), v)

    @pl.when(j == nj - 1)
    def _():
        new_keys = n_new * DIFF_HEADS
        key_tok = lax.broadcasted_iota(jnp.int32, (rows, new_keys), 1) // DIFF_HEADS
        causal = key_tok <= lax.broadcasted_iota(jnp.int32, (rows, new_keys), 0) % n_new
        vn = vn_ref[...].astype(bf16)
        for w in range(2):
            s = logits(w, kn_ref[w].astype(bf16), new_keys, key_tok[:1].astype(f32))
            update(w, jnp.where(causal, s, -jnp.inf), vn)
        o0 = acc[0] / l_scr[0][:, :1]
        o1 = acc[1] / l_scr[1][:, :1]
        o_ref[...] = _diff_finish(o0, o1, _diff_lambda(lam_ref), g_ref[...])


def _diff_attn_sample(q, k_new, v_new, cache_k, cache_v, page_table, lam4, subln_g):
    B, Ts = q.shape[:2]
    n_pages = page_table.shape[1]
    page = cache_k.shape[1]
    pages = SAMPLE_PAGES_PER_STEP
    rows = DIFF_HEADS * Ts
    by_parity = lambda x, order: jnp.transpose(x.reshape(B, Ts, DIFF_HEADS, 2, DIFF_HD), order).reshape(B, 2, rows, DIFF_HD)
    qw = by_parity(q, (0, 3, 2, 1, 4))
    kw = by_parity(k_new, (0, 3, 1, 2, 4))
    vn = v_new.reshape(B, rows, 2 * DIFF_HD)
    ck = cache_k.reshape(cache_k.shape[0], page * DIFF_MAPS, DIFF_HD)
    par = pl.BlockSpec((None, 2, rows, DIFF_HD), lambda b, j, pt: (b, 0, 0, 0))
    kspec = lambda p: pl.BlockSpec((None, page * DIFF_MAPS, DIFF_HD), lambda b, j, pt: (pt[b, j * pages + p], 0, 0))
    vspec = lambda p: pl.BlockSpec((None, page, DIFF_HEADS, 2 * DIFF_HD),
                                   lambda b, j, pt: (pt[b, j * pages + p], 0, 0, 0))
    grid_spec = pltpu.PrefetchScalarGridSpec(
        num_scalar_prefetch=1,
        grid=(B, n_pages // pages),
        in_specs=[par, par, pl.BlockSpec((None, rows, 2 * DIFF_HD), lambda b, j, pt: (b, 0, 0))]
                 + [kspec(p) for p in range(pages)] + [vspec(p) for p in range(pages)]
                 + [pl.BlockSpec((4, DIFF_HD), lambda b, j, pt: (0, 0)),
                    pl.BlockSpec((1, 2 * DIFF_HD), lambda b, j, pt: (0, 0))],
        out_specs=pl.BlockSpec((None, rows, 2 * DIFF_HD), lambda b, j, pt: (b, 0, 0)),
        scratch_shapes=[pltpu.VMEM((2, rows, LANES), f32), pltpu.VMEM((2, rows, LANES), f32),
                        pltpu.VMEM((2, rows, 2 * DIFF_HD), f32)],
    )
    o = pl.pallas_call(
        functools.partial(_paged_attn_kernel, pages=pages, page=page, n_new=Ts),
        grid_spec=grid_spec,
        out_shape=jax.ShapeDtypeStruct((B, rows, 2 * DIFF_HD), f32),
        compiler_params=pltpu.CompilerParams(dimension_semantics=("parallel", "arbitrary"),
                                             vmem_limit_bytes=VMEM_LIMIT_BYTES),
        name="diff_attn_sample",
    )(page_table, qw, kw, vn, *([ck] * pages), *([cache_v] * pages), lam4, subln_g.reshape(1, -1))
    return jnp.transpose(o.reshape(B, DIFF_HEADS, Ts, 2 * DIFF_HD), (0, 2, 1, 3))


def _split(x):
    hi = x.astype(bf16)
    lo = (x - hi.astype(f32)).astype(bf16)
    return hi, lo


def _dot3(a, b, dims=_NN):
    dg = lambda x, y: lax.dot_general(x, y, dims, preferred_element_type=f32)
    return dg(a[0], b[0]) + dg(a[0], b[1]) + dg(a[1], b[0])


def _split3_dot(x, ones_b, ones_first):
    hi = x.astype(bf16)
    r1 = x - hi.astype(f32)
    mid = r1.astype(bf16)
    lo = (r1 - mid.astype(f32)).astype(bf16)
    if ones_first:
        dn = lambda y: lax.dot_general(ones_b, y, _NN, preferred_element_type=f32)
    else:
        dn = lambda y: lax.dot_general(y, ones_b, _NN, preferred_element_type=f32)
    return dn(hi) + dn(mid) + dn(lo)


def _head_sums(x):
    same_head = (lax.broadcasted_iota(jnp.int32, (HEAD_PAIR_LANES, HEAD_PAIR_LANES), 0) // RWKV_HD
                 == lax.broadcasted_iota(jnp.int32, (HEAD_PAIR_LANES, HEAD_PAIR_LANES), 1) // RWKV_HD).astype(bf16)
    return jnp.concatenate(
        [_split3_dot(x[:, p * HEAD_PAIR_LANES:(p + 1) * HEAD_PAIR_LANES], same_head, False)
         for p in range(x.shape[1] // HEAD_PAIR_LANES)], axis=1)


def _rwkv_pre_kernel(p_ref, sh_ref, mu_ref, w0_ref, a0_ref, kk_ref, ka_ref, ww_ref, wa_ref, wg_ref,
                     r_ref, lw_ref, k_ref, v_ref, kkn_ref, a_ref, g_ref, carry):
    t = pl.program_id(1)

    @pl.when(t == 0)
    def _():
        carry[...] = sh_ref[...]

    p = p_ref[...]
    rows = p.shape[0]
    prev = jnp.concatenate([carry[...], p[:rows - 1]], axis=0)
    carry[...] = p[rows - 1:rows]
    xs = p + (prev - p) * mu_ref[...]
    k = xs[:, RWKV_W:2 * RWKV_W]
    lora = xs[:, LORA_LO:LORA_LO + LORA_WIN]
    gate = xs[:, GATE_LO:]
    dn = lambda x, w_ref: lax.dot_general(x.astype(bf16), w_ref[...], _NN, preferred_element_type=f32)
    z = -(w0_ref[...] + dn(jnp.tanh(lora), ww_ref))
    w_log = -(jnp.maximum(z, 0.0) + jnp.log(1.0 + jnp.exp(-jnp.abs(z)))) - 0.5
    a = 1.0 / (1.0 + jnp.exp(-(a0_ref[...] + dn(lora, wa_ref))))
    kk = k * kk_ref[...]
    norm = jnp.maximum(jnp.sqrt(_head_sums(kk * kk)), 1e-12)
    r_ref[...] = xs[:, :RWKV_W]
    lw_ref[...] = -jnp.exp(w_log)
    k_ref[...] = k * (1.0 + (a - 1.0) * ka_ref[...])
    v_ref[...] = xs[:, 2 * RWKV_W:3 * RWKV_W]
    kkn_ref[...] = kk / norm
    a_ref[...] = a
    g_ref[...] = dn(1.0 / (1.0 + jnp.exp(-gate)), wg_ref)


def _rwkv_pre(p, shift_prev, lp):
    B, T, _ = p.shape
    bt = min(T, RWKV_ROWS)
    pad_rows = lambda w, lo, n: jnp.zeros((n, RWKV_W), f32).at[lo:lo + w.shape[0]].set(w).astype(bf16)
    ww = pad_rows(lp['w_w2'], 0, LORA_WIN)
    wa = pad_rows(lp['w_a2'], DECAY_LORA, LORA_WIN)
    wg = pad_rows(lp['w_g2'], GATE_WIN - GATE_LORA, GATE_WIN)
    vec = lambda x: x.reshape(1, -1)
    const = lambda shape: pl.BlockSpec(shape, lambda b, t: (0,) * len(shape))
    seq = pl.BlockSpec((None, bt, RWKV_W), lambda b, t: (b, t, 0))
    return pl.pallas_call(
        _rwkv_pre_kernel,
        grid=(B, T // bt),
        in_specs=[pl.BlockSpec((None, bt, RWKV_COLS), lambda b, t: (b, t, 0)),
                  pl.BlockSpec((None, 1, RWKV_COLS), lambda b, t: (b, 0, 0)),
                  const((1, RWKV_COLS))] + [const((1, RWKV_W))] * 4
                 + [const((LORA_WIN, RWKV_W)), const((LORA_WIN, RWKV_W)), const((GATE_WIN, RWKV_W))],
        out_specs=[seq] * 7,
        out_shape=[jax.ShapeDtypeStruct((B, T, RWKV_W), f32)] * 7,
        scratch_shapes=[pltpu.VMEM((1, RWKV_COLS), f32)],
        compiler_params=pltpu.CompilerParams(dimension_semantics=("parallel", "arbitrary"),
                                             vmem_limit_bytes=VMEM_LIMIT_BYTES),
        name="rwkv_pre",
    )(p, shift_prev.reshape(B, 1, RWKV_COLS), vec(lp['shift_mu']), vec(lp['w0']), vec(lp['a0']), vec(lp['k_k']),
      vec(lp['k_a']), ww, wa, wg)


def _rwkv_post_kernel(y_ref, r_ref, k_ref, v_ref, g_ref, lw_ref, lb_ref, rk_ref, o_ref):
    y = y_ref[...]
    inv = 1.0 / RWKV_HD
    d = y - _head_sums(y) * inv
    var = _head_sums(d * d) * inv
    yn = d * lax.rsqrt(var + GN_EPS) * lw_ref[...] + lb_ref[...]
    bonus = _head_sums(r_ref[...] * k_ref[...] * rk_ref[...]) * v_ref[...]
    o_ref[...] = (yn + bonus) * g_ref[...]


def _rwkv_post(y, r, k, v, g, lp):
    B, T, W = y.shape
    bt = min(T, RWKV_ROWS)
    seq = pl.BlockSpec((None, bt, W), lambda b, t: (b, t, 0))
    const = pl.BlockSpec((1, W), lambda b, t: (0, 0))
    return pl.pallas_call(
        _rwkv_post_kernel,
        grid=(B, T // bt),
        in_specs=[seq] * 5 + [const] * 3,
        out_specs=seq,
        out_shape=jax.ShapeDtypeStruct((B, T, W), f32),
        compiler_params=pltpu.CompilerParams(dimension_semantics=("parallel", "parallel")),
        name="rwkv_post",
    )(y, r, k, v, g, lp['lnx_w'].reshape(1, W), lp['lnx_b'].reshape(1, W), lp['r_k'].reshape(1, W))


def _rwkv_scan_kernel(r_ref, lw_ref, k_ref, v_ref, kk_ref, a_ref, s0_ref, y_ref, sT_ref, state, *, pairs):
    c = pl.program_id(1)

    @pl.when(c == 0)
    def _():
        state[...] = s0_ref[...]

    C = r_ref.shape[0]
    C2 = 2 * C
    row = lax.broadcasted_iota(jnp.int32, (C2, C2), 0)
    col = lax.broadcasted_iota(jnp.int32, (C2, C2), 1)
    tril_strict = col < row
    tril_incl = col <= row
    eye = row == col
    eye_keys = (lax.broadcasted_iota(jnp.int32, (HEAD_PAIR_LANES, HEAD_PAIR_LANES), 0)
                == lax.broadcasted_iota(jnp.int32, (HEAD_PAIR_LANES, HEAD_PAIR_LANES), 1))
    first_head = lax.broadcasted_iota(jnp.int32, (C, HEAD_PAIR_LANES), 1) < RWKV_HD
    cum = (lax.broadcasted_iota(jnp.int32, (C, C), 1) <= lax.broadcasted_iota(jnp.int32, (C, C), 0)).astype(bf16)

    def stack(x):
        return jnp.concatenate([jnp.where(first_head, x, 0.0), jnp.where(first_head, 0.0, x)], axis=0)

    P = range(pairs)
    sls = [slice(p * HEAD_PAIR_LANES, (p + 1) * HEAD_PAIR_LANES) for p in P]
    each = lambda fn, *ls: [fn(*xs) for xs in zip(*ls)]

    lw = [lw_ref[:, sl] for sl in sls]
    cs = each(lambda x: _split3_dot(x, cum, True), lw)
    cl = [x[C - 1:C, :] for x in cs]
    kk = [kk_ref[:, sl] for sl in sls]
    kv = [k_ref[:, sl] for sl in sls]
    b = each(lambda x, sl: x * a_ref[:, sl], kk, sls)
    e_neg = each(lambda x: jnp.exp(-x), cs)
    e_end = each(lambda x, y: jnp.exp(x - y), cl, cs)
    A_s = each(lambda x, c_, l_: stack(-x * jnp.exp(c_ - l_)), kk, cs, lw)
    R_s = each(lambda sl, c_: stack(r_ref[:, sl] * jnp.exp(c_)), sls, cs)
    sB = each(lambda x, e: _split(stack(x * e)), b, e_neg)
    sK = each(lambda x, e: _split(stack(x * e)), kv, e_neg)
    sV = each(lambda sl: _split(stack(v_ref[:, sl])), sls)
    sA = each(_split, A_s)
    sR = each(_split, R_s)
    BhT = each(lambda x, e: stack(x * e).T, b, e_end)
    KhT = each(lambda x, e: stack(x * e).T, kv, e_end)
    L = each(lambda x, y: jnp.where(tril_strict, _dot3(x, y, _NT), 0.0), sA, sB)
    Mak = each(lambda x, y: jnp.where(tril_strict, _dot3(x, y, _NT), 0.0), sA, sK)
    Mrb = each(lambda x, y: jnp.where(tril_incl, _dot3(x, y, _NT), 0.0), sR, sB)
    Mrk = each(lambda x, y: jnp.where(tril_incl, _dot3(x, y, _NT), 0.0), sR, sK)
    T = each(lambda x: jnp.where(eye, 1.0, 0.0).astype(f32) + x, L)
    Lp = L
    n = 2
    while n < C:
        sLp = each(_split, Lp)
        Lp = each(lambda x: _dot3(x, x), sLp)
        T = each(lambda t, lp_: t + _dot3(_split(t), _split(lp_)), T, Lp)
        n *= 2
    X1 = each(lambda m, v: _dot3(_split(m), v), Mak, sV)
    sPU = each(lambda t, a_, x: _split(_dot3(_split(t), _split(jnp.concatenate([a_, x], axis=1)))), T, A_s, X1)
    QY = each(lambda m, pu: _dot3(_split(m), pu), Mrb, sPU)
    Q = each(lambda r_, qy: r_ + qy[:, :HEAD_PAIR_LANES], R_s, QY)
    Yl = each(lambda qy, m, v: qy[:, HEAD_PAIR_LANES:] + _dot3(_split(m), v), QY, Mrk, sV)
    GH = each(lambda bt_, pu: _dot3(_split(bt_), pu), BhT, sPU)
    G = each(lambda c_, gh: jnp.where(eye_keys, jnp.exp(c_), 0.0) + gh[:, :HEAD_PAIR_LANES], cl, GH)
    H = each(lambda gh, kt, v: gh[:, HEAD_PAIR_LANES:] + _dot3(_split(kt), v), GH, KhT, sV)
    sS = [_split(state[p]) for p in P]
    Y = each(lambda q, s, yl: _dot3(_split(q), s) + yl, Q, sS, Yl)
    S_new = each(lambda g, s, h: _dot3(_split(g), s) + h, G, sS, H)
    for p in P:
        state[p] = S_new[p]
        y_ref[:, sls[p]] = Y[p][:C] + Y[p][C:]

    @pl.when(c == pl.num_programs(1) - 1)
    def _():
        sT_ref[...] = state[...]


def _rwkv_scan(r, lw, k, v, kk, a, s0, chunk):
    B, T, W = r.shape
    pairs = W // HEAD_PAIR_LANES
    seq = pl.BlockSpec((None, chunk, W), lambda b_, c: (b_, c, 0))
    st = pl.BlockSpec((None, pairs, HEAD_PAIR_LANES, HEAD_PAIR_LANES), lambda b_, c: (b_, 0, 0, 0))
    return pl.pallas_call(
        functools.partial(_rwkv_scan_kernel, pairs=pairs),
        grid=(B, T // chunk),
        in_specs=[seq] * 6 + [st],
        out_specs=[seq, st],
        out_shape=[jax.ShapeDtypeStruct((B, T, W), f32), jax.ShapeDtypeStruct(s0.shape, f32)],
        scratch_shapes=[pltpu.VMEM((pairs, HEAD_PAIR_LANES, HEAD_PAIR_LANES), f32)],
        compiler_params=pltpu.CompilerParams(dimension_semantics=("parallel", "arbitrary")),
        name="rwkv_scan",
    )(r, lw, k, v, kk, a, s0)


def _state_to_stacked(S):
    B, H, N, _ = S.shape
    St = jnp.swapaxes(S, -1, -2).reshape(B, H // 2, 2, N, N)
    z = jnp.zeros_like(St[:, :, 0])
    top = jnp.concatenate([St[:, :, 0], z], axis=-1)
    bot = jnp.concatenate([z, St[:, :, 1]], axis=-1)
    return jnp.concatenate([top, bot], axis=-2)


def _stacked_to_state(Ss):
    B, P, _, _ = Ss.shape
    N = RWKV_HD
    St = jnp.stack([Ss[:, :, :N, :N], Ss[:, :, N:, N:]], axis=2).reshape(B, 2 * P, N, N)
    return jnp.swapaxes(St, -1, -2)


def _rwkv7_mix(p, shift_prev, wkv0, lp):
    B, T, _ = p.shape
    r, lw, k, v, kk, a, g = _rwkv_pre(p, shift_prev, lp)
    chunk = min(RWKV_CHUNK, -(-T // SUBLANES) * SUBLANES)
    Tp = -(-T // chunk) * chunk
    pad = lambda x: jnp.pad(x, ((0, 0), (0, Tp - T), (0, 0)))
    y, s_end = _rwkv_scan(pad(r), pad(lw), pad(k), pad(v), pad(kk), pad(a), _state_to_stacked(wkv0), chunk)
    out = _rwkv_post(y[:, :T], r, k, v, g, lp)
    return out, p[:, -1], _stacked_to_state(s_end)


def _softmax_rows(s):
    e = jnp.exp(s - jnp.max(s, axis=-1, keepdims=True))
    return e / jnp.sum(e, axis=-1, keepdims=True)


def _cross_prompt_kernel(q_ref, k_ref, v_ref, o_ref):
    for h in range(MEM_HEADS):
        sl = slice(h * MEM_HD, (h + 1) * MEM_HD)
        s = lax.dot_general(q_ref[:, sl].astype(bf16), k_ref[:, sl].astype(bf16), _NT, preferred_element_type=f32)
        p = _softmax_rows(s * (MEM_HD ** -0.5))
        o_ref[:, sl] = lax.dot_general(p.astype(bf16), v_ref[:, sl].astype(bf16), _NN, preferred_element_type=f32)


def _cross_attn_prompt(q, mk, mv):
    T, D = q.shape
    M = mk.shape[0]
    bt = min(T, CROSS_ROWS)
    mem = pl.BlockSpec((M, D), lambda i: (0, 0))
    return pl.pallas_call(
        _cross_prompt_kernel,
        grid=(T // bt,),
        in_specs=[pl.BlockSpec((bt, D), lambda i: (i, 0)), mem, mem],
        out_specs=pl.BlockSpec((bt, D), lambda i: (i, 0)),
        out_shape=jax.ShapeDtypeStruct((T, D), f32),
        compiler_params=pltpu.CompilerParams(dimension_semantics=("parallel",), vmem_limit_bytes=VMEM_LIMIT_BYTES),
        name="cross_attn_prompt",
    )(q, mk, mv)


def _cross_sample_kernel(q_ref, k_ref, v_ref, o_ref):
    ts = q_ref.shape[0]
    n = k_ref.shape[0] * MEM_HEADS
    rows = MEM_HEADS * ts
    q = jnp.concatenate([q_ref[:, h * MEM_HD:(h + 1) * MEM_HD] for h in range(MEM_HEADS)], axis=0).astype(bf16)
    k = k_ref[...].reshape(n, MEM_HD).astype(bf16)
    v = v_ref[...].reshape(n, MEM_HD).astype(bf16)
    s = lax.dot_general(q, k, _NT, preferred_element_type=f32) * (MEM_HD ** -0.5)
    own = ((lax.broadcasted_iota(jnp.int32, (rows, n), 0) // ts)
           == (lax.broadcasted_iota(jnp.int32, (rows, n), 1) % MEM_HEADS))
    p = _softmax_rows(jnp.where(own, s, -jnp.inf))
    o = lax.dot_general(p.astype(bf16), v, _NN, preferred_element_type=f32)
    for h in range(MEM_HEADS):
        o_ref[:, h * MEM_HD:(h + 1) * MEM_HD] = o[h * ts:(h + 1) * ts]


def _cross_attn_sample(q, mk, mv):
    B, Ts, D = q.shape
    M = mk.shape[1]
    mem = pl.BlockSpec((None, M, MEM_HEADS, MEM_HD), lambda b: (b, 0, 0, 0))
    tok = pl.BlockSpec((None, Ts, D), lambda b: (b, 0, 0))
    return pl.pallas_call(
        _cross_sample_kernel,
        grid=(B,),
        in_specs=[tok, mem, mem],
        out_specs=tok,
        out_shape=jax.ShapeDtypeStruct((B, Ts, D), f32),
        compiler_params=pltpu.CompilerParams(dimension_semantics=("parallel",), vmem_limit_bytes=VMEM_LIMIT_BYTES),
        name="cross_attn_sample",
    )(q, mk, mv)


def _peer_route_kernel(h_ref, g_ref, wq_ref, keys_ref, xn_ref, r2_ref, e2_ref, e1_ref, wd_ref,
                       q_scr, top_scr, cand_scr):
    x = h_ref[...]
    xn = x * lax.rsqrt(jnp.mean(x * x, axis=-1, keepdims=True) + NORM_EPS) * g_ref[...]
    xb = xn.astype(bf16)
    xn_ref[...] = xb
    q_scr[...] = lax.dot_general(xb, wq_ref[...], _NN, preferred_element_type=f32)
    neg = -jnp.inf
    bt = x.shape[0]
    key_id = lax.broadcasted_iota(jnp.int32, (N_KEYS, bt), 0)
    cand_id = lax.broadcasted_iota(jnp.int32, (CAND_ROWS, bt), 0)

    def pop_max(x_, ids):
        m = jnp.max(x_, axis=0, keepdims=True)
        hit = ids == jnp.min(jnp.where(x_ == m, ids, ids.shape[0]), axis=0, keepdims=True)
        return m, hit, jnp.where(hit, neg, x_)

    def per_head(h, carry):
        ranks = []
        scores = []
        for c in range(2):
            off = pl.multiple_of((h * 2 + c) * (PEER_DK // 2), LANES)
            qhc = q_scr[:, pl.ds(off, PEER_DK // 2)].astype(bf16)
            s = lax.dot_general(keys_ref[h, c], qhc, _NT, preferred_element_type=f32)
            scores.append(s)
            x_ = s
            rank = jnp.full(s.shape, PEER_TOPK, f32)
            for i in range(PEER_TOPK):
                m, hit, x_ = pop_max(x_, key_id)
                top_scr[c, i:i + 1, :] = m
                rank = jnp.where(hit, jnp.float32(i), rank)
            ranks.append(rank)
        s1, s2 = scores
        for i in range(PEER_TOPK):
            n_i = PEER_TOPK // (i + 1)
            cand_scr[CAND_OFFSETS[i]:CAND_OFFSETS[i] + n_i, :] = top_scr[0, i:i + 1, :] + top_scr[1, 0:n_i, :]
        cand_scr[CAND_OFFSETS[PEER_TOPK]:, :] = jnp.full((CAND_ROWS - CAND_OFFSETS[PEER_TOPK], bt), neg, f32)
        best = top_scr[0, 0:1, :] + top_scr[1, 0:1, :]
        x_ = cand_scr[...]
        z = jnp.zeros_like(best)
        for i in range(PEER_TOPK):
            m, _, x_ = pop_max(x_, cand_id)
            z = z + jnp.exp(m - best)
        taken = (x_ == neg).astype(f32)
        width = jnp.zeros(s1.shape, f32)
        for i in range(PEER_TOPK):
            n_i = PEER_TOPK // (i + 1)
            t_i = jnp.sum(taken[CAND_OFFSETS[i]:CAND_OFFSETS[i] + n_i], axis=0, keepdims=True)
            width = jnp.where(ranks[0] == jnp.float32(i), t_i, width)
        r2_ref[h] = ranks[1]
        e2_ref[h] = jnp.exp(s2 - top_scr[1, 0:1, :])
        e1_ref[h] = jnp.exp(s1 - top_scr[0, 0:1, :]) / z
        wd_ref[h] = width
        return carry

    lax.fori_loop(0, PEER_HEADS, per_head, 0)


def _peer_route(h, g, wq_b, keys_b):
    T, D = h.shape
    bt = _peer_token_block(T)
    tab = jax.ShapeDtypeStruct((PEER_HEADS, N_KEYS, T), f32)
    tab_spec = pl.BlockSpec((PEER_HEADS, N_KEYS, bt), lambda i: (0, 0, i))
    return pl.pallas_call(
        _peer_route_kernel,
        grid=(T // bt,),
        in_specs=[pl.BlockSpec((bt, D), lambda i: (i, 0)), pl.BlockSpec((1, D), lambda i: (0, 0)),
                  pl.BlockSpec((D, PEER_HEADS * PEER_DK), lambda i: (0, 0), pipeline_mode=pl.Buffered(1)),
                  pl.BlockSpec((PEER_HEADS, 2, N_KEYS, PEER_DK // 2), lambda i: (0, 0, 0, 0))],
        out_specs=[pl.BlockSpec((bt, D), lambda i: (i, 0))] + [tab_spec] * 4,
        out_shape=[jax.ShapeDtypeStruct((T, D), bf16)] + [tab] * 4,
        scratch_shapes=[pltpu.VMEM((bt, PEER_HEADS * PEER_DK), f32), pltpu.VMEM((2, PEER_TOPK, bt), f32),
                        pltpu.VMEM((CAND_ROWS, bt), f32)],
        compiler_params=pltpu.CompilerParams(dimension_semantics=("parallel",), vmem_limit_bytes=VMEM_LIMIT_BYTES),
        name="peer_route",
    )(h, g.reshape(1, D), wq_b, keys_b)


def _peer_mix_kernel(xn_ref, u_ref, v_ref, r2_ref, e2_ref, e1_ref, wd_ref, h_ref, g_ref, y_ref):
    j = pl.program_id(1)

    @pl.when(j == 0)
    def _():
        y_ref[...] = h_ref[...]

    per = PEER_EXPERT_CHUNK // PEER_MIX_SLICES
    xn = xn_ref[...]
    hts = [lax.dot_general(u_ref[k * per:(k + 1) * per, :], xn, _NT, preferred_element_type=f32)
           for k in range(PEER_MIX_SLICES)]
    total = None
    for k in range(PEER_MIX_SLICES):
        ht = hts[k]
        hid = 0.5 * ht * (1.0 + lax.erf(ht * (1.0 / math.sqrt(2.0))))
        rows = []
        for a in range(k * per // N_KEYS, (k + 1) * per // N_KEYS):
            g = None
            for h in range(PEER_HEADS):
                t = jnp.where(r2_ref[h] < wd_ref[h, a:a + 1, :], e2_ref[h] * e1_ref[h, a:a + 1, :], 0.0)
                g = t if g is None else g + t
            rows.append(g)
        gate = jnp.concatenate(rows, axis=0)
        w = (gate * hid).T.astype(bf16)
        d = lax.dot_general(w, v_ref[k * per:(k + 1) * per, :], _NN, preferred_element_type=f32)
        total = d if total is None else total + d
    y_ref[...] += total

    @pl.when(j == pl.num_programs(1) - 1)
    def _():
        x = y_ref[...]
        y_ref[...] = x * lax.rsqrt(jnp.mean(x * x, axis=-1, keepdims=True) + NORM_EPS) * g_ref[...]


def _peer_mix(xn_b, u_b, v_b, r2, e2, e1, wd, h, final_g):
    T, D = h.shape
    E = u_b.shape[0]
    bt = _peer_token_block(T)
    EC = PEER_EXPERT_CHUNK
    tok1 = pl.BlockSpec((bt, D), lambda i, j: (i, 0), pipeline_mode=pl.Buffered(1))
    wts = pl.BlockSpec((EC, D), lambda i, j: (j, 0))
    full = pl.BlockSpec((PEER_HEADS, N_KEYS, bt), lambda i, j: (0, 0, i), pipeline_mode=pl.Buffered(1))
    part = pl.BlockSpec((PEER_HEADS, EC // N_KEYS, bt), lambda i, j: (0, j, i))
    return pl.pallas_call(
        _peer_mix_kernel,
        grid=(T // bt, E // EC),
        in_specs=[tok1, wts, wts, full, full, part, part, tok1, pl.BlockSpec((1, D), lambda i, j: (0, 0))],
        out_specs=pl.BlockSpec((bt, D), lambda i, j: (i, 0)),
        out_shape=jax.ShapeDtypeStruct((T, D), f32),
        compiler_params=pltpu.CompilerParams(dimension_semantics=("parallel", "arbitrary"),
                                             vmem_limit_bytes=VMEM_LIMIT_BYTES),
        name="peer_mix",
    )(xn_b, u_b, v_b, r2, e2, e1, wd, h, final_g.reshape(1, D))


def kernel(x_prompt, x_sample, cache_k, cache_v, state_wkv, state_shift, cache_mem_k, cache_mem_v, page_table, mem_prompt, norm1_g, w_in, lam_q1, lam_k1, lam_q2, lam_k2, subln_g, shift_mu, w0, w_w2, a0, w_a2, w_g2, k_k, k_a, r_k, lnx_w, lnx_b, w_out, norm2_g, mem_norm_g, w_cq, w_mk, w_mv, w_co, norm3_g, w_pq, peer_keys, peer_u, peer_v, final_g):
    l = 0
    D = D_MODEL
    Bp, Tp, _ = x_prompt.shape
    Bs, Ts, _ = x_sample.shape
    assert Bp == 1, "the prompt group is handled as one sequence"
    lp = dict(shift_mu=shift_mu[l], w0=w0[l], w_w2=w_w2[l], a0=a0[l], w_a2=w_a2[l], w_g2=w_g2[l], k_k=k_k[l],
              k_a=k_a[l], r_k=r_k[l], lnx_w=lnx_w[l], lnx_b=lnx_b[l])
    w_qkv = w_in[l][:, :3 * DIFF_W].astype(bf16)
    w_rwkv = w_in[l][:, 3 * DIFF_W:].astype(bf16)
    w_out_b, w_cq_b, w_co_b = w_out[l].astype(bf16), w_cq[l].astype(bf16), w_co[l].astype(bf16)
    w_mk_b, w_mv_b, w_pq_b = w_mk[l].astype(bf16), w_mv[l].astype(bf16), w_pq[l].astype(bf16)
    keys_b, u_b, v_b = peer_keys[l].astype(bf16), peer_u[l].astype(bf16), peer_v[l].astype(bf16)
    lam4 = jnp.stack([lam_q1[l], lam_k1[l], lam_q2[l], lam_k2[l]])

    xp = x_prompt.reshape(Bp * Tp, D)
    xs = x_sample.reshape(Bs * Ts, D)

    mem = mem_prompt.reshape(-1, D)
    mk_p = _matmul(mem, w_mk_b, gain=mem_norm_g[l], name="mem_k")
    mv_p = _matmul(mem, w_mv_b, gain=mem_norm_g[l], name="mem_v")

    qkv_p = _matmul(xp, w_qkv, gain=norm1_g[l], name="in_proj_attn")
    rw_p = _matmul(xp, w_rwkv, gain=norm1_g[l], name="in_proj_rwkv")
    a_p = _diff_attn_prompt(qkv_p, lam4, subln_g[l])
    r_p, sh_p, wkv_p = _rwkv7_mix(rw_p.reshape(Bp, Tp, RWKV_COLS), jnp.zeros((Bp, RWKV_COLS), f32),
                                  jnp.zeros((Bp, RWKV_HEADS, RWKV_HD, RWKV_HD), f32), lp)
    h_p = _matmul(jnp.concatenate([a_p, r_p.reshape(Bp * Tp, RWKV_W)], axis=-1), w_out_b, residual=xp, name="out_proj")

    qkv_s = _matmul(xs, w_qkv, gain=norm1_g[l], name="in_proj_attn")
    rw_s = _matmul(xs, w_rwkv, gain=norm1_g[l], name="in_proj_rwkv")
    q_s = qkv_s[:, :DIFF_W].reshape(Bs, Ts, DIFF_MAPS, DIFF_HD)
    k_s = qkv_s[:, DIFF_W:2 * DIFF_W].reshape(Bs, Ts, DIFF_MAPS, DIFF_HD)
    v_s = qkv_s[:, 2 * DIFF_W:].reshape(Bs, Ts, DIFF_HEADS, 2 * DIFF_HD)
    a_s = _diff_attn_sample(q_s, k_s, v_s, cache_k[l], cache_v[l], page_table, lam4, subln_g[l])
    r_s, sh_s, wkv_s = _rwkv7_mix(rw_s.reshape(Bs, Ts, RWKV_COLS), state_shift[l], state_wkv[l], lp)
    h_s = _matmul(jnp.concatenate([a_s.reshape(Bs * Ts, DIFF_W), r_s.reshape(Bs * Ts, RWKV_W)], axis=-1), w_out_b,
                  residual=xs, name="out_proj")

    cq_p = _matmul(h_p, w_cq_b, gain=norm2_g[l], name="cross_q")
    cq_s = _matmul(h_s, w_cq_b, gain=norm2_g[l], name="cross_q")
    co_p = _cross_attn_prompt(cq_p, mk_p, mv_p)
    co_s = _cross_attn_sample(cq_s.reshape(Bs, Ts, D), cache_mem_k[l], cache_mem_v[l]).reshape(Bs * Ts, D)
    h_p = _matmul(co_p, w_co_b, residual=h_p, name="cross_out")
    h_s = _matmul(co_s, w_co_b, residual=h_s, name="cross_out")

    h_all = jnp.concatenate([h_p, h_s], axis=0)
    xn_b, r2, e2, e1, wd = _peer_route(h_all, norm3_g[l], w_pq_b, keys_b)
    y_all = _peer_mix(xn_b, u_b, v_b, r2, e2, e1, wd, h_all, final_g)
    y_prompt = y_all[:Bp * Tp].reshape(Bp, Tp, D)
    y_sample = y_all[Bp * Tp:].reshape(Bs, Ts, D)

    k_p = qkv_p[:, DIFF_W:2 * DIFF_W].reshape(Bp, Tp, DIFF_MAPS, DIFF_HD)
    v_p = qkv_p[:, 2 * DIFF_W:].reshape(Bp, Tp, DIFF_HEADS, 2 * DIFF_HD)
    mem_shape = (Bp, -1, MEM_HEADS, MEM_HD)
    return (y_prompt, y_sample, k_p[None], v_p[None], k_s[None], v_s[None],
            wkv_p[None], sh_p[None], wkv_s[None], sh_s[None], mk_p.reshape(mem_shape)[None], mv_p.reshape(mem_shape)[None])
```

```python
import functools
import math

import jax
import jax.numpy as jnp
from jax import lax
from jax.experimental import pallas as pl
from jax.experimental.pallas import tpu as pltpu

f32, bf16 = jnp.float32, jnp.bfloat16

D_MODEL = 2048
DIFF_HD = 128
DIFF_HEADS = 4
DIFF_MAPS = 8
DIFF_W = 1024
SUBLN_EPS = 1e-5
LAM_INIT = 0.8 - 0.6 * math.exp(-0.3 * 0)
RWKV_HD = 64
RWKV_W = 1024
RWKV_HEADS = 16
DECAY_LORA = 96
ICLR_LORA = 96
GATE_LORA = 256
RWKV_COLS = 3 * RWKV_W + DECAY_LORA + ICLR_LORA + GATE_LORA
GN_EPS = 64e-5
MEM_HEADS = 4
MEM_HD = D_MODEL // MEM_HEADS
PEER_HEADS = 8
N_KEYS = 128
PEER_TOPK = 16
PEER_DK = 256
NORM_EPS = 1e-6

VMEM_LIMIT_BYTES = 60 << 20
LANES = 128
SUBLANES = 8

RWKV_CHUNK = 64
HEAD_PAIR_LANES = 2 * RWKV_HD
RWKV_ROWS = 256
FLASH_BLOCK = 1024
SAMPLE_PAGES_PER_STEP = 8
LOG2E = math.log2(math.e)
CROSS_ROWS = 512
PEER_TOKEN_BLOCK_MAX = 640
PEER_EXPERT_CHUNK = 1024
PEER_MIX_SLICES = 2
MATMUL_ROWS = 512
CAND_OFFSETS = [sum(PEER_TOPK // (k + 1) for k in range(i)) for i in range(PEER_TOPK + 1)]
CAND_ROWS = -(-CAND_OFFSETS[PEER_TOPK] // SUBLANES) * SUBLANES

LORA_LO = 3 * RWKV_W
LORA_WIN = 2 * LANES
GATE_LO = RWKV_COLS - GATE_LORA - RWKV_HD
GATE_WIN = RWKV_COLS - GATE_LO


def _peer_token_block(n_tokens):
    return max(b for b in range(LANES, PEER_TOKEN_BLOCK_MAX + 1, LANES) if n_tokens % b == 0)


_NN = (((1,), (0,)), ((), ()))
_NT = (((1,), (1,)), ((), ()))


def _matmul_kernel(*refs, norm, residual):
    x_ref, refs = refs[0], refs[1:]
    if norm:
        g_ref, refs = refs[0], refs[1:]
    w_ref, refs = refs[0], refs[1:]
    if residual:
        r_ref, refs = refs[0], refs[1:]
    o_ref = refs[0]
    x = x_ref[...]
    if norm:
        x = x * lax.rsqrt(jnp.mean(x * x, axis=-1, keepdims=True) + NORM_EPS) * g_ref[...]
    y = lax.dot_general(x.astype(bf16), w_ref[...], _NN, preferred_element_type=f32)
    if residual:
        y = r_ref[...] + y
    o_ref[...] = y


def _matmul(x, w_b, gain=None, residual=None, name="matmul"):
    M, K = x.shape
    N = w_b.shape[1]
    bm = min(MATMUL_ROWS, M)
    operands = [x]
    specs = [pl.BlockSpec((bm, K), lambda i: (i, 0))]
    if gain is not None:
        operands.append(gain.reshape(1, K))
        specs.append(pl.BlockSpec((1, K), lambda i: (0, 0)))
    operands.append(w_b)
    specs.append(pl.BlockSpec((K, N), lambda i: (0, 0), pipeline_mode=pl.Buffered(1)))
    if residual is not None:
        operands.append(residual)
        specs.append(pl.BlockSpec((bm, N), lambda i: (i, 0)))
    return pl.pallas_call(
        functools.partial(_matmul_kernel, norm=gain is not None, residual=residual is not None),
        grid=(M // bm,),
        in_specs=specs,
        out_specs=pl.BlockSpec((bm, N), lambda i: (i, 0)),
        out_shape=jax.ShapeDtypeStruct((M, N), f32),
        compiler_params=pltpu.CompilerParams(dimension_semantics=("parallel",), vmem_limit_bytes=VMEM_LIMIT_BYTES),
        name=name,
    )(*operands)


def _diff_lambda(lam_ref):
    lq1, lk1, lq2, lk2 = (lam_ref[i:i + 1, :] for i in range(4))
    return (jnp.exp(jnp.sum(lq1 * lk1, axis=-1, keepdims=True))
            - jnp.exp(jnp.sum(lq2 * lk2, axis=-1, keepdims=True)) + LAM_INIT)


def _diff_finish(o0, o1, lam, g):
    o = o0 - lam * o1
    o = o * lax.rsqrt(jnp.mean(o * o, axis=-1, keepdims=True) + SUBLN_EPS) * g
    return o * (1.0 - LAM_INIT)


def _flash_prompt_kernel(qi_ref, kj_ref, q_ref, k_ref, v_ref, lam_ref, g_ref, o_ref, m_scr, l_scr, acc, bias_scr, *,
                         blk):
    h = pl.program_id(0)
    t = pl.program_id(1)
    i = qi_ref[t]
    j = kj_ref[t]
    slopes = [jnp.exp2(-(2 * h + m + 1).astype(f32)) * LOG2E for m in range(2)]

    @pl.when(j == 0)
    def _():
        m_scr[...] = jnp.full(m_scr.shape, -jnp.inf, f32)
        l_scr[...] = jnp.zeros(l_scr.shape, f32)
        acc[...] = jnp.zeros(acc.shape, f32)
        back = (lax.broadcasted_iota(jnp.int32, (blk, blk), 0)
                - lax.broadcasted_iota(jnp.int32, (blk, blk), 1)).astype(f32)
        for m in range(2):
            bias_scr[m, 0] = -slopes[m] * back
            bias_scr[m, 1] = jnp.where(back >= 0, -slopes[m] * back, -jnp.inf)

    diag = (j == i).astype(jnp.int32)
    v = v_ref[...].astype(bf16)
    for m in range(2):
        shift = -slopes[m] * ((i - j) * blk).astype(f32)
        q = (q_ref[:, m * DIFF_HD:(m + 1) * DIFF_HD] * (DIFF_HD ** -0.5 * LOG2E)).astype(bf16)
        k = k_ref[:, m * DIFF_HD:(m + 1) * DIFF_HD].astype(bf16)
        s = lax.dot_general(q, k, _NT, preferred_element_type=f32) + bias_scr[m, diag]
        cols = [s[:, c * LANES:(c + 1) * LANES] for c in range(blk // LANES)]
        m_old = m_scr[m]
        m_new = jnp.maximum(m_old, jnp.max(functools.reduce(jnp.maximum, cols), axis=-1, keepdims=True) + shift)
        level = m_new - shift
        ps = [jnp.exp2(c - level) for c in cols]
        alpha = jnp.exp2(m_old - m_new)
        l_scr[m] = alpha * l_scr[m] + jnp.sum(functools.reduce(jnp.add, ps), axis=-1, keepdims=True)
        pv = lax.dot_general(jnp.concatenate(ps, axis=1).astype(bf16), v, _NN, preferred_element_type=f32)
        acc[m] = jnp.concatenate([alpha, alpha], axis=1) * acc[m] + pv
        m_scr[m] = m_new

    @pl.when(j == i)
    def _():
        inv = [1.0 / l_scr[m] for m in range(2)]
        o0 = acc[0] * jnp.concatenate([inv[0], inv[0]], axis=1)
        o1 = acc[1] * jnp.concatenate([inv[1], inv[1]], axis=1)
        o_ref[...] = _diff_finish(o0, o1, _diff_lambda(lam_ref), g_ref[...])


def _diff_attn_prompt(qkv, lam4, subln_g):
    T = qkv.shape[0]
    blk = FLASH_BLOCK
    nb = T // blk
    W = 2 * DIFF_HD
    pairs = [(i, j) for i in range(nb) for j in range(i + 1)]
    qi = jnp.asarray([p[0] for p in pairs], jnp.int32)
    kj = jnp.asarray([p[1] for p in pairs], jnp.int32)
    grid_spec = pltpu.PrefetchScalarGridSpec(
        num_scalar_prefetch=2,
        grid=(DIFF_HEADS, len(pairs)),
        in_specs=[pl.BlockSpec((blk, W), lambda h, t, qi, kj: (qi[t], h)),
                  pl.BlockSpec((blk, W), lambda h, t, qi, kj: (kj[t], DIFF_HEADS + h)),
                  pl.BlockSpec((blk, W), lambda h, t, qi, kj: (kj[t], 2 * DIFF_HEADS + h)),
                  pl.BlockSpec((4, DIFF_HD), lambda h, t, qi, kj: (0, 0)),
                  pl.BlockSpec((1, W), lambda h, t, qi, kj: (0, 0))],
        out_specs=pl.BlockSpec((blk, W), lambda h, t, qi, kj: (qi[t], h)),
        scratch_shapes=[pltpu.VMEM((2, blk, LANES), f32), pltpu.VMEM((2, blk, LANES), f32),
                        pltpu.VMEM((2, blk, W), f32), pltpu.VMEM((2, 2, blk, blk), f32)],
    )
    return pl.pallas_call(
        functools.partial(_flash_prompt_kernel, blk=blk),
        grid_spec=grid_spec,
        out_shape=jax.ShapeDtypeStruct((T, DIFF_HEADS * W), f32),
        compiler_params=pltpu.CompilerParams(dimension_semantics=("parallel", "arbitrary"),
                                             vmem_limit_bytes=VMEM_LIMIT_BYTES),
        name="diff_attn_prompt",
    )(qi, kj, qkv, qkv, qkv, lam4, subln_g.reshape(1, W))


def _paged_attn_kernel(pt_ref, q_ref, kn_ref, vn_ref, *refs, pages, page, n_new):
    k_refs, v_refs = refs[:pages], refs[pages:2 * pages]
    lam_ref, g_ref, o_ref, m_scr, l_scr, acc = refs[2 * pages:]
    j = pl.program_id(1)
    nj = pl.num_programs(1)
    rows = DIFF_HEADS * n_new
    span = pages * page
    past = nj * span

    @pl.when(j == 0)
    def _():
        m_scr[...] = jnp.full(m_scr.shape, -jnp.inf, f32)
        l_scr[...] = jnp.zeros(l_scr.shape, f32)
        acc[...] = jnp.zeros(acc.shape, f32)

    def update(w, s, v):
        m_old = m_scr[w]
        m_new = jnp.maximum(m_old, jnp.max(s, axis=-1, keepdims=True))
        p = jnp.exp(s - m_new[:, :1])
        alpha = jnp.exp(m_old - m_new)
        l_scr[w] = alpha * l_scr[w] + jnp.sum(p, axis=-1, keepdims=True)
        acc[w] = alpha[:, :1] * acc[w] + lax.dot_general(p.astype(bf16), v, _NN, preferred_element_type=f32)
        m_scr[w] = m_new

    def logits(w, k, n_keys, key_pos):
        row_head = lax.broadcasted_iota(jnp.int32, (rows, n_keys), 0) // n_new
        lane_head = lax.broadcasted_iota(jnp.int32, (rows, n_keys), 1) % DIFF_HEADS
        slope = jnp.exp2(-(2 * row_head + w + 1).astype(f32))
        q = (q_ref[w] * (DIFF_HD ** -0.5)).astype(bf16)
        s = lax.dot_general(q, k, _NT, preferred_element_type=f32) + slope * key_pos
        return jnp.where(row_head == lane_head, s, -jnp.inf)

    n_keys = span * DIFF_HEADS
    key_pos = ((lax.broadcasted_iota(jnp.int32, (1, n_keys), 1) // DIFF_HEADS) + (j * span - past)).astype(f32)
    v = jnp.concatenate([v_refs[p][...].reshape(page * DIFF_HEADS, 2 * DIFF_HD) for p in range(pages)],
                        axis=0).astype(bf16)
    for w in range(2):
        k = jnp.concatenate([k_refs[p][pl.ds(w, page * DIFF_HEADS, stride=2), :] for p in range(pages)],
                            axis=0).astype(bf16)
        update(w, logits(w, k, n_keys, key_pos), v)

    @pl.when(j == nj - 1)
    def _():
        new_keys = n_new * DIFF_HEADS
        key_tok = lax.broadcasted_iota(jnp.int32, (rows, new_keys), 1) // DIFF_HEADS
        causal = key_tok <= lax.broadcasted_iota(jnp.int32, (rows, new_keys), 0) % n_new
        vn = vn_ref[...].astype(bf16)
        for w in range(2):
            s = logits(w, kn_ref[w].astype(bf16), new_keys, key_tok[:1].astype(f32))
            update(w, jnp.where(causal, s, -jnp.inf), vn)
        o0 = acc[0] / l_scr[0][:, :1]
        o1 = acc[1] / l_scr[1][:, :1]
        o_ref[...] = _diff_finish(o0, o1, _diff_lambda(lam_ref), g_ref[...])


def _diff_attn_sample(q, k_new, v_new, cache_k, cache_v, page_table, lam4, subln_g):
    B, Ts = q.shape[:2]
    n_pages = page_table.shape[1]
    page = cache_k.shape[1]
    pages = SAMPLE_PAGES_PER_STEP
    rows = DIFF_HEADS * Ts
    by_parity = lambda x, order: jnp.transpose(x.reshape(B, Ts, DIFF_HEADS, 2, DIFF_HD), order).reshape(B, 2, rows, DIFF_HD)
    qw = by_parity(q, (0, 3, 2, 1, 4))
    kw = by_parity(k_new, (0, 3, 1, 2, 4))
    vn = v_new.reshape(B, rows, 2 * DIFF_HD)
    ck = cache_k.reshape(cache_k.shape[0], page * DIFF_MAPS, DIFF_HD)
    par = pl.BlockSpec((None, 2, rows, DIFF_HD), lambda b, j, pt: (b, 0, 0, 0))
    kspec = lambda p: pl.BlockSpec((None, page * DIFF_MAPS, DIFF_HD), lambda b, j, pt: (pt[b, j * pages + p], 0, 0))
    vspec = lambda p: pl.BlockSpec((None, page, DIFF_HEADS, 2 * DIFF_HD),
                                   lambda b, j, pt: (pt[b, j * pages + p], 0, 0, 0))
    grid_spec = pltpu.PrefetchScalarGridSpec(
        num_scalar_prefetch=1,
        grid=(B, n_pages // pages),
        in_specs=[par, par, pl.BlockSpec((None, rows, 2 * DIFF_HD), lambda b, j, pt: (b, 0, 0))]
                 + [kspec(p) for p in range(pages)] + [vspec(p) for p in range(pages)]
                 + [pl.BlockSpec((4, DIFF_HD), lambda b, j, pt: (0, 0)),
                    pl.BlockSpec((1, 2 * DIFF_HD), lambda b, j, pt: (0, 0))],
        out_specs=pl.BlockSpec((None, rows, 2 * DIFF_HD), lambda b, j, pt: (b, 0, 0)),
        scratch_shapes=[pltpu.VMEM((2, rows, LANES), f32), pltpu.VMEM((2, rows, LANES), f32),
                        pltpu.VMEM((2, rows, 2 * DIFF_HD), f32)],
    )
    o = pl.pallas_call(
        functools.partial(_paged_attn_kernel, pages=pages, page=page, n_new=Ts),
        grid_spec=grid_spec,
        out_shape=jax.ShapeDtypeStruct((B, rows, 2 * DIFF_HD), f32),
        compiler_params=pltpu.CompilerParams(dimension_semantics=("parallel", "arbitrary"),
                                             vmem_limit_bytes=VMEM_LIMIT_BYTES),
        name="diff_attn_sample",
    )(page_table, qw, kw, vn, *([ck] * pages), *([cache_v] * pages), lam4, subln_g.reshape(1, -1))
    return jnp.transpose(o.reshape(B, DIFF_HEADS, Ts, 2 * DIFF_HD), (0, 2, 1, 3))


def _split(x):
    hi = x.astype(bf16)
    lo = (x - hi.astype(f32)).astype(bf16)
    return hi, lo


def _dot3(a, b, dims=_NN):
    dg = lambda x, y: lax.dot_general(x, y, dims, preferred_element_type=f32)
    return dg(a[0], b[0]) + dg(a[0], b[1]) + dg(a[1], b[0])


def _split3_dot(x, ones_b, ones_first):
    hi = x.astype(bf16)
    r1 = x - hi.astype(f32)
    mid = r1.astype(bf16)
    lo = (r1 - mid.astype(f32)).astype(bf16)
    if ones_first:
        dn = lambda y: lax.dot_general(ones_b, y, _NN, preferred_element_type=f32)
    else:
        dn = lambda y: lax.dot_general(y, ones_b, _NN, preferred_element_type=f32)
    return dn(hi) + dn(mid) + dn(lo)


def _head_sums(x):
    same_head = (lax.broadcasted_iota(jnp.int32, (HEAD_PAIR_LANES, HEAD_PAIR_LANES), 0) // RWKV_HD
                 == lax.broadcasted_iota(jnp.int32, (HEAD_PAIR_LANES, HEAD_PAIR_LANES), 1) // RWKV_HD).astype(bf16)
    return jnp.concatenate(
        [_split3_dot(x[:, p * HEAD_PAIR_LANES:(p + 1) * HEAD_PAIR_LANES], same_head, False)
         for p in range(x.shape[1] // HEAD_PAIR_LANES)], axis=1)


def _rwkv_pre_kernel(p_ref, sh_ref, mu_ref, w0_ref, a0_ref, kk_ref, ka_ref, ww_ref, wa_ref, wg_ref,
                     r_ref, lw_ref, k_ref, v_ref, kkn_ref, a_ref, g_ref, carry):
    t = pl.program_id(1)

    @pl.when(t == 0)
    def _():
        carry[...] = sh_ref[...]

    p = p_ref[...]
    rows = p.shape[0]
    prev = jnp.concatenate([carry[...], p[:rows - 1]], axis=0)
    carry[...] = p[rows - 1:rows]
    xs = p + (prev - p) * mu_ref[...]
    k = xs[:, RWKV_W:2 * RWKV_W]
    lora = xs[:, LORA_LO:LORA_LO + LORA_WIN]
    gate = xs[:, GATE_LO:]
    dn = lambda x, w_ref: lax.dot_general(x.astype(bf16), w_ref[...], _NN, preferred_element_type=f32)
    z = -(w0_ref[...] + dn(jnp.tanh(lora), ww_ref))
    w_log = -(jnp.maximum(z, 0.0) + jnp.log(1.0 + jnp.exp(-jnp.abs(z)))) - 0.5
    a = 1.0 / (1.0 + jnp.exp(-(a0_ref[...] + dn(lora, wa_ref))))
    kk = k * kk_ref[...]
    norm = jnp.maximum(jnp.sqrt(_head_sums(kk * kk)), 1e-12)
    r_ref[...] = xs[:, :RWKV_W]
    lw_ref[...] = -jnp.exp(w_log)
    k_ref[...] = k * (1.0 + (a - 1.0) * ka_ref[...])
    v_ref[...] = xs[:, 2 * RWKV_W:3 * RWKV_W]
    kkn_ref[...] = kk / norm
    a_ref[...] = a
    g_ref[...] = dn(1.0 / (1.0 + jnp.exp(-gate)), wg_ref)


def _rwkv_pre(p, shift_prev, lp):
    B, T, _ = p.shape
    bt = min(T, RWKV_ROWS)
    pad_rows = lambda w, lo, n: jnp.zeros((n, RWKV_W), f32).at[lo:lo + w.shape[0]].set(w).astype(bf16)
    ww = pad_rows(lp['w_w2'], 0, LORA_WIN)
    wa = pad_rows(lp['w_a2'], DECAY_LORA, LORA_WIN)
    wg = pad_rows(lp['w_g2'], GATE_WIN - GATE_LORA, GATE_WIN)
    vec = lambda x: x.reshape(1, -1)
    const = lambda shape: pl.BlockSpec(shape, lambda b, t: (0,) * len(shape))
    seq = pl.BlockSpec((None, bt, RWKV_W), lambda b, t: (b, t, 0))
    return pl.pallas_call(
        _rwkv_pre_kernel,
        grid=(B, T // bt),
        in_specs=[pl.BlockSpec((None, bt, RWKV_COLS), lambda b, t: (b, t, 0)),
                  pl.BlockSpec((None, 1, RWKV_COLS), lambda b, t: (b, 0, 0)),
                  const((1, RWKV_COLS))] + [const((1, RWKV_W))] * 4
                 + [const((LORA_WIN, RWKV_W)), const((LORA_WIN, RWKV_W)), const((GATE_WIN, RWKV_W))],
        out_specs=[seq] * 7,
        out_shape=[jax.ShapeDtypeStruct((B, T, RWKV_W), f32)] * 7,
        scratch_shapes=[pltpu.VMEM((1, RWKV_COLS), f32)],
        compiler_params=pltpu.CompilerParams(dimension_semantics=("parallel", "arbitrary"),
                                             vmem_limit_bytes=VMEM_LIMIT_BYTES),
        name="rwkv_pre",
    )(p, shift_prev.reshape(B, 1, RWKV_COLS), vec(lp['shift_mu']), vec(lp['w0']), vec(lp['a0']), vec(lp['k_k']),
      vec(lp['k_a']), ww, wa, wg)


def _rwkv_post_kernel(y_ref, r_ref, k_ref, v_ref, g_ref, lw_ref, lb_ref, rk_ref, o_ref):
    y = y_ref[...]
    inv = 1.0 / RWKV_HD
    d = y - _head_sums(y) * inv
    var = _head_sums(d * d) * inv
    yn = d * lax.rsqrt(var + GN_EPS) * lw_ref[...] + lb_ref[...]
    bonus = _head_sums(r_ref[...] * k_ref[...] * rk_ref[...]) * v_ref[...]
    o_ref[...] = (yn + bonus) * g_ref[...]


def _rwkv_post(y, r, k, v, g, lp):
    B, T, W = y.shape
    bt = min(T, RWKV_ROWS)
    seq = pl.BlockSpec((None, bt, W), lambda b, t: (b, t, 0))
    const = pl.BlockSpec((1, W), lambda b, t: (0, 0))
    return pl.pallas_call(
        _rwkv_post_kernel,
        grid=(B, T // bt),
        in_specs=[seq] * 5 + [const] * 3,
        out_specs=seq,
        out_shape=jax.ShapeDtypeStruct((B, T, W), f32),
        compiler_params=pltpu.CompilerParams(dimension_semantics=("parallel", "parallel")),
        name="rwkv_post",
    )(y, r, k, v, g, lp['lnx_w'].reshape(1, W), lp['lnx_b'].reshape(1, W), lp['r_k'].reshape(1, W))


def _rwkv_scan_kernel(r_ref, lw_ref, k_ref, v_ref, kk_ref, a_ref, s0_ref, y_ref, sT_ref, state, *, pairs):
    c = pl.program_id(1)

    @pl.when(c == 0)
    def _():
        state[...] = s0_ref[...]

    C = r_ref.shape[0]
    C2 = 2 * C
    row = lax.broadcasted_iota(jnp.int32, (C2, C2), 0)
    col = lax.broadcasted_iota(jnp.int32, (C2, C2), 1)
    tril_strict = col < row
    tril_incl = col <= row
    eye = row == col
    eye_keys = (lax.broadcasted_iota(jnp.int32, (HEAD_PAIR_LANES, HEAD_PAIR_LANES), 0)
                == lax.broadcasted_iota(jnp.int32, (HEAD_PAIR_LANES, HEAD_PAIR_LANES), 1))
    first_head = lax.broadcasted_iota(jnp.int32, (C, HEAD_PAIR_LANES), 1) < RWKV_HD
    cum = (lax.broadcasted_iota(jnp.int32, (C, C), 1) <= lax.broadcasted_iota(jnp.int32, (C, C), 0)).astype(bf16)

    def stack(x):
        return jnp.concatenate([jnp.where(first_head, x, 0.0), jnp.where(first_head, 0.0, x)], axis=0)

    P = range(pairs)
    sls = [slice(p * HEAD_PAIR_LANES, (p + 1) * HEAD_PAIR_LANES) for p in P]
    each = lambda fn, *ls: [fn(*xs) for xs in zip(*ls)]

    lw = [lw_ref[:, sl] for sl in sls]
    cs = each(lambda x: _split3_dot(x, cum, True), lw)
    cl = [x[C - 1:C, :] for x in cs]
    kk = [kk_ref[:, sl] for sl in sls]
    kv = [k_ref[:, sl] for sl in sls]
    b = each(lambda x, sl: x * a_ref[:, sl], kk, sls)
    e_neg = each(lambda x: jnp.exp(-x), cs)
    e_end = each(lambda x, y: jnp.exp(x - y), cl, cs)
    A_s = each(lambda x, c_, l_: stack(-x * jnp.exp(c_ - l_)), kk, cs, lw)
    R_s = each(lambda sl, c_: stack(r_ref[:, sl] * jnp.exp(c_)), sls, cs)
    sB = each(lambda x, e: _split(stack(x * e)), b, e_neg)
    sK = each(lambda x, e: _split(stack(x * e)), kv, e_neg)
    sV = each(lambda sl: _split(stack(v_ref[:, sl])), sls)
    sA = each(_split, A_s)
    sR = each(_split, R_s)
    BhT = each(lambda x, e: stack(x * e).T, b, e_end)
    KhT = each(lambda x, e: stack(x * e).T, kv, e_end)
    L = each(lambda x, y: jnp.where(tril_strict, _dot3(x, y, _NT), 0.0), sA, sB)
    Mak = each(lambda x, y: jnp.where(tril_strict, _dot3(x, y, _NT), 0.0), sA, sK)
    Mrb = each(lambda x, y: jnp.where(tril_incl, _dot3(x, y, _NT), 0.0), sR, sB)
    Mrk = each(lambda x, y: jnp.where(tril_incl, _dot3(x, y, _NT), 0.0), sR, sK)
    T = each(lambda x: jnp.where(eye, 1.0, 0.0).astype(f32) + x, L)
    Lp = L
    n = 2
    while n < C:
        sLp = each(_split, Lp)
        Lp = each(lambda x: _dot3(x, x), sLp)
        T = each(lambda t, lp_: t + _dot3(_split(t), _split(lp_)), T, Lp)
        n *= 2
    X1 = each(lambda m, v: _dot3(_split(m), v), Mak, sV)
    sPU = each(lambda t, a_, x: _split(_dot3(_split(t), _split(jnp.concatenate([a_, x], axis=1)))), T, A_s, X1)
    QY = each(lambda m, pu: _dot3(_split(m), pu), Mrb, sPU)
    Q = each(lambda r_, qy: r_ + qy[:, :HEAD_PAIR_LANES], R_s, QY)
    Yl = each(lambda qy, m, v: qy[:, HEAD_PAIR_LANES:] + _dot3(_split(m), v), QY, Mrk, sV)
    GH = each(lambda bt_, pu: _dot3(_split(bt_), pu), BhT, sPU)
    G = each(lambda c_, gh: jnp.where(eye_keys, jnp.exp(c_), 0.0) + gh[:, :HEAD_PAIR_LANES], cl, GH)
    H = each(lambda gh, kt, v: gh[:, HEAD_PAIR_LANES:] + _dot3(_split(kt), v), GH, KhT, sV)
    sS = [_split(state[p]) for p in P]
    Y = each(lambda q, s, yl: _dot3(_split(q), s) + yl, Q, sS, Yl)
    S_new = each(lambda g, s, h: _dot3(_split(g), s) + h, G, sS, H)
    for p in P:
        state[p] = S_new[p]
        y_ref[:, sls[p]] = Y[p][:C] + Y[p][C:]

    @pl.when(c == pl.num_programs(1) - 1)
    def _():
        sT_ref[...] = state[...]


def _rwkv_scan(r, lw, k, v, kk, a, s0, chunk):
    B, T, W = r.shape
    pairs = W // HEAD_PAIR_LANES
    seq = pl.BlockSpec((None, chunk, W), lambda b_, c: (b_, c, 0))
    st = pl.BlockSpec((None, pairs, HEAD_PAIR_LANES, HEAD_PAIR_LANES), lambda b_, c: (b_, 0, 0, 0))
    return pl.pallas_call(
        functools.partial(_rwkv_scan_kernel, pairs=pairs),
        grid=(B, T // chunk),
        in_specs=[seq] * 6 + [st],
        out_specs=[seq, st],
        out_shape=[jax.ShapeDtypeStruct((B, T, W), f32), jax.ShapeDtypeStruct(s0.shape, f32)],
        scratch_shapes=[pltpu.VMEM((pairs, HEAD_PAIR_LANES, HEAD_PAIR_LANES), f32)],
        compiler_params=pltpu.CompilerParams(dimension_semantics=("parallel", "arbitrary")),
        name="rwkv_scan",
    )(r, lw, k, v, kk, a, s0)


def _state_to_stacked(S):
    B, H, N, _ = S.shape
    St = jnp.swapaxes(S, -1, -2).reshape(B, H // 2, 2, N, N)
    z = jnp.zeros_like(St[:, :, 0])
    top = jnp.concatenate([St[:, :, 0], z], axis=-1)
    bot = jnp.concatenate([z, St[:, :, 1]], axis=-1)
    return jnp.concatenate([top, bot], axis=-2)


def _stacked_to_state(Ss):
    B, P, _, _ = Ss.shape
    N = RWKV_HD
    St = jnp.stack([Ss[:, :, :N, :N], Ss[:, :, N:, N:]], axis=2).reshape(B, 2 * P, N, N)
    return jnp.swapaxes(St, -1, -2)


def _rwkv7_mix(p, shift_prev, wkv0, lp):
    B, T, _ = p.shape
    r, lw, k, v, kk, a, g = _rwkv_pre(p, shift_prev, lp)
    chunk = min(RWKV_CHUNK, -(-T // SUBLANES) * SUBLANES)
    Tp = -(-T // chunk) * chunk
    pad = lambda x: jnp.pad(x, ((0, 0), (0, Tp - T), (0, 0)))
    y, s_end = _rwkv_scan(pad(r), pad(lw), pad(k), pad(v), pad(kk), pad(a), _state_to_stacked(wkv0), chunk)
    out = _rwkv_post(y[:, :T], r, k, v, g, lp)
    return out, p[:, -1], _stacked_to_state(s_end)


def _softmax_rows(s):
    e = jnp.exp(s - jnp.max(s, axis=-1, keepdims=True))
    return e / jnp.sum(e, axis=-1, keepdims=True)


def _cross_prompt_kernel(q_ref, k_ref, v_ref, o_ref):
    for h in range(MEM_HEADS):
        sl = slice(h * MEM_HD, (h + 1) * MEM_HD)
        s = lax.dot_general(q_ref[:, sl].astype(bf16), k_ref[:, sl].astype(bf16), _NT, preferred_element_type=f32)
        p = _softmax_rows(s * (MEM_HD ** -0.5))
        o_ref[:, sl] = lax.dot_general(p.astype(bf16), v_ref[:, sl].astype(bf16), _NN, preferred_element_type=f32)


def _cross_attn_prompt(q, mk, mv):
    T, D = q.shape
    M = mk.shape[0]
    bt = min(T, CROSS_ROWS)
    mem = pl.BlockSpec((M, D), lambda i: (0, 0))
    return pl.pallas_call(
        _cross_prompt_kernel,
        grid=(T // bt,),
        in_specs=[pl.BlockSpec((bt, D), lambda i: (i, 0)), mem, mem],
        out_specs=pl.BlockSpec((bt, D), lambda i: (i, 0)),
        out_shape=jax.ShapeDtypeStruct((T, D), f32),
        compiler_params=pltpu.CompilerParams(dimension_semantics=("parallel",), vmem_limit_bytes=VMEM_LIMIT_BYTES),
        name="cross_attn_prompt",
    )(q, mk, mv)


def _cross_sample_kernel(q_ref, k_ref, v_ref, o_ref):
    ts = q_ref.shape[0]
    n = k_ref.shape[0] * MEM_HEADS
    rows = MEM_HEADS * ts
    q = jnp.concatenate([q_ref[:, h * MEM_HD:(h + 1) * MEM_HD] for h in range(MEM_HEADS)], axis=0).astype(bf16)
    k = k_ref[...].reshape(n, MEM_HD).astype(bf16)
    v = v_ref[...].reshape(n, MEM_HD).astype(bf16)
    s = lax.dot_general(q, k, _NT, preferred_element_type=f32) * (MEM_HD ** -0.5)
    own = ((lax.broadcasted_iota(jnp.int32, (rows, n), 0) // ts)
           == (lax.broadcasted_iota(jnp.int32, (rows, n), 1) % MEM_HEADS))
    p = _softmax_rows(jnp.where(own, s, -jnp.inf))
    o = lax.dot_general(p.astype(bf16), v, _NN, preferred_element_type=f32)
    for h in range(MEM_HEADS):
        o_ref[:, h * MEM_HD:(h + 1) * MEM_HD] = o[h * ts:(h + 1) * ts]


def _cross_attn_sample(q, mk, mv):
    B, Ts, D = q.shape
    M = mk.shape[1]
    mem = pl.BlockSpec((None, M, MEM_HEADS, MEM_HD), lambda b: (b, 0, 0, 0))
    tok = pl.BlockSpec((None, Ts, D), lambda b: (b, 0, 0))
    return pl.pallas_call(
        _cross_sample_kernel,
        grid=(B,),
        in_specs=[tok, mem, mem],
        out_specs=tok,
        out_shape=jax.ShapeDtypeStruct((B, Ts, D), f32),
        compiler_params=pltpu.CompilerParams(dimension_semantics=("parallel",), vmem_limit_bytes=VMEM_LIMIT_BYTES),
        name="cross_attn_sample",
    )(q, mk, mv)


def _peer_route_kernel(h_ref, g_ref, wq_ref, keys_ref, xn_ref, r2_ref, e2_ref, e1_ref, wd_ref,
                       q_scr, top_scr, cand_scr):
    x = h_ref[...]
    xn = x * lax.rsqrt(jnp.mean(x * x, axis=-1, keepdims=True) + NORM_EPS) * g_ref[...]
    xb = xn.astype(bf16)
    xn_ref[...] = xb
    q_scr[...] = lax.dot_general(xb, wq_ref[...], _NN, preferred_element_type=f32)
    neg = -jnp.inf
    bt = x.shape[0]
    key_id = lax.broadcasted_iota(jnp.int32, (N_KEYS, bt), 0)
    cand_id = lax.broadcasted_iota(jnp.int32, (CAND_ROWS, bt), 0)

    def pop_max(x_, ids):
        m = jnp.max(x_, axis=0, keepdims=True)
        hit = ids == jnp.min(jnp.where(x_ == m, ids, ids.shape[0]), axis=0, keepdims=True)
        return m, hit, jnp.where(hit, neg, x_)

    def per_head(h, carry):
        ranks = []
        scores = []
        for c in range(2):
            off = pl.multiple_of((h * 2 + c) * (PEER_DK // 2), LANES)
            qhc = q_scr[:, pl.ds(off, PEER_DK // 2)].astype(bf16)
            s = lax.dot_general(keys_ref[h, c], qhc, _NT, preferred_element_type=f32)
            scores.append(s)
            x_ = s
            rank = jnp.full(s.shape, PEER_TOPK, f32)
            for i in range(PEER_TOPK):
                m, hit, x_ = pop_max(x_, key_id)
                top_scr[c, i:i + 1, :] = m
                rank = jnp.where(hit, jnp.float32(i), rank)
            ranks.append(rank)
        s1, s2 = scores
        for i in range(PEER_TOPK):
            n_i = PEER_TOPK // (i + 1)
            cand_scr[CAND_OFFSETS[i]:CAND_OFFSETS[i] + n_i, :] = top_scr[0, i:i + 1, :] + top_scr[1, 0:n_i, :]
        cand_scr[CAND_OFFSETS[PEER_TOPK]:, :] = jnp.full((CAND_ROWS - CAND_OFFSETS[PEER_TOPK], bt), neg, f32)
        best = top_scr[0, 0:1, :] + top_scr[1, 0:1, :]
        x_ = cand_scr[...]
        z = jnp.zeros_like(best)
        for i in range(PEER_TOPK):
            m, _, x_ = pop_max(x_, cand_id)
            z = z + jnp.exp(m - best)
        taken = (x_ == neg).astype(f32)
        width = jnp.zeros(s1.shape, f32)
        for i in range(PEER_TOPK):
            n_i = PEER_TOPK // (i + 1)
            t_i = jnp.sum(taken[CAND_OFFSETS[i]:CAND_OFFSETS[i] + n_i], axis=0, keepdims=True)
            width = jnp.where(ranks[0] == jnp.float32(i), t_i, width)
        r2_ref[h] = ranks[1]
        e2_ref[h] = jnp.exp(s2 - top_scr[1, 0:1, :])
        e1_ref[h] = jnp.exp(s1 - top_scr[0, 0:1, :]) / z
        wd_ref[h] = width
        return carry

    lax.fori_loop(0, PEER_HEADS, per_head, 0)


def _peer_route(h, g, wq_b, keys_b):
    T, D = h.shape
    bt = _peer_token_block(T)
    tab = jax.ShapeDtypeStruct((PEER_HEADS, N_KEYS, T), f32)
    tab_spec = pl.BlockSpec((PEER_HEADS, N_KEYS, bt), lambda i: (0, 0, i))
    return pl.pallas_call(
        _peer_route_kernel,
        grid=(T // bt,),
        in_specs=[pl.BlockSpec((bt, D), lambda i: (i, 0)), pl.BlockSpec((1, D), lambda i: (0, 0)),
                  pl.BlockSpec((D, PEER_HEADS * PEER_DK), lambda i: (0, 0), pipeline_mode=pl.Buffered(1)),
                  pl.BlockSpec((PEER_HEADS, 2, N_KEYS, PEER_DK // 2), lambda i: (0, 0, 0, 0))],
        out_specs=[pl.BlockSpec((bt, D), lambda i: (i, 0))] + [tab_spec] * 4,
        out_shape=[jax.ShapeDtypeStruct((T, D), bf16)] + [tab] * 4,
        scratch_shapes=[pltpu.VMEM((bt, PEER_HEADS * PEER_DK), f32), pltpu.VMEM((2, PEER_TOPK, bt), f32),
                        pltpu.VMEM((CAND_ROWS, bt), f32)],
        compiler_params=pltpu.CompilerParams(dimension_semantics=("parallel",), vmem_limit_bytes=VMEM_LIMIT_BYTES),
        name="peer_route",
    )(h, g.reshape(1, D), wq_b, keys_b)


def _peer_mix_kernel(xn_ref, u_ref, v_ref, r2_ref, e2_ref, e1_ref, wd_ref, h_ref, g_ref, y_ref):
    j = pl.program_id(1)

    @pl.when(j == 0)
    def _():
        y_ref[...] = h_ref[...]

    per = PEER_EXPERT_CHUNK // PEER_MIX_SLICES
    xn = xn_ref[...]
    hts = [lax.dot_general(u_ref[k * per:(k + 1) * per, :], xn, _NT, preferred_element_type=f32)
           for k in range(PEER_MIX_SLICES)]
    total = None
    for k in range(PEER_MIX_SLICES):
        ht = hts[k]
        hid = 0.5 * ht * (1.0 + lax.erf(ht * (1.0 / math.sqrt(2.0))))
        rows = []
        for a in range(k * per // N_KEYS, (k + 1) * per // N_KEYS):
            g = None
            for h in range(PEER_HEADS):
                t = jnp.where(r2_ref[h] < wd_ref[h, a:a + 1, :], e2_ref[h] * e1_ref[h, a:a + 1, :], 0.0)
                g = t if g is None else g + t
            rows.append(g)
        gate = jnp.concatenate(rows, axis=0)
        w = (gate * hid).T.astype(bf16)
        d = lax.dot_general(w, v_ref[k * per:(k + 1) * per, :], _NN, preferred_element_type=f32)
        total = d if total is None else total + d
    y_ref[...] += total

    @pl.when(j == pl.num_programs(1) - 1)
    def _():
        x = y_ref[...]
        y_ref[...] = x * lax.rsqrt(jnp.mean(x * x, axis=-1, keepdims=True) + NORM_EPS) * g_ref[...]


def _peer_mix(xn_b, u_b, v_b, r2, e2, e1, wd, h, final_g):
    T, D = h.shape
    E = u_b.shape[0]
    bt = _peer_token_block(T)
    EC = PEER_EXPERT_CHUNK
    tok1 = pl.BlockSpec((bt, D), lambda i, j: (i, 0), pipeline_mode=pl.Buffered(1))
    wts = pl.BlockSpec((EC, D), lambda i, j: (j, 0))
    full = pl.BlockSpec((PEER_HEADS, N_KEYS, bt), lambda i, j: (0, 0, i), pipeline_mode=pl.Buffered(1))
    part = pl.BlockSpec((PEER_HEADS, EC // N_KEYS, bt), lambda i, j: (0, j, i))
    return pl.pallas_call(
        _peer_mix_kernel,
        grid=(T // bt, E // EC),
        in_specs=[tok1, wts, wts, full, full, part, part, tok1, pl.BlockSpec((1, D), lambda i, j: (0, 0))],
        out_specs=pl.BlockSpec((bt, D), lambda i, j: (i, 0)),
        out_shape=jax.ShapeDtypeStruct((T, D), f32),
        compiler_params=pltpu.CompilerParams(dimension_semantics=("parallel", "arbitrary"),
                                             vmem_limit_bytes=VMEM_LIMIT_BYTES),
        name="peer_mix",
    )(xn_b, u_b, v_b, r2, e2, e1, wd, h, final_g.reshape(1, D))


def kernel(x_prompt, x_sample, cache_k, cache_v, state_wkv, state_shift, cache_mem_k, cache_mem_v, page_table, mem_prompt, norm1_g, w_in, lam_q1, lam_k1, lam_q2, lam_k2, subln_g, shift_mu, w0, w_w2, a0, w_a2, w_g2, k_k, k_a, r_k, lnx_w, lnx_b, w_out, norm2_g, mem_norm_g, w_cq, w_mk, w_mv, w_co, norm3_g, w_pq, peer_keys, peer_u, peer_v, final_g):
    l = 0
    D = D_MODEL
    Bp, Tp, _ = x_prompt.shape
    Bs, Ts, _ = x_sample.shape
    assert Bp == 1, "the prompt group is handled as one sequence"
    lp = dict(shift_mu=shift_mu[l], w0=w0[l], w_w2=w_w2[l], a0=a0[l], w_a2=w_a2[l], w_g2=w_g2[l], k_k=k_k[l],
              k_a=k_a[l], r_k=r_k[l], lnx_w=lnx_w[l], lnx_b=lnx_b[l])
    w_qkv = w_in[l][:, :3 * DIFF_W].astype(bf16)
    w_rwkv = w_in[l][:, 3 * DIFF_W:].astype(bf16)
    w_out_b, w_cq_b, w_co_b = w_out[l].astype(bf16), w_cq[l].astype(bf16), w_co[l].astype(bf16)
    w_mk_b, w_mv_b, w_pq_b = w_mk[l].astype(bf16), w_mv[l].astype(bf16), w_pq[l].astype(bf16)
    keys_b, u_b, v_b = peer_keys[l].astype(bf16), peer_u[l].astype(bf16), peer_v[l].astype(bf16)
    lam4 = jnp.stack([lam_q1[l], lam_k1[l], lam_q2[l], lam_k2[l]])

    xp = x_prompt.reshape(Bp * Tp, D)
    xs = x_sample.reshape(Bs * Ts, D)

    mem = mem_prompt.reshape(-1, D)
    mk_p = _matmul(mem, w_mk_b, gain=mem_norm_g[l], name="mem_k")
    mv_p = _matmul(mem, w_mv_b, gain=mem_norm_g[l], name="mem_v")

    qkv_p = _matmul(xp, w_qkv, gain=norm1_g[l], name="in_proj_attn")
    rw_p = _matmul(xp, w_rwkv, gain=norm1_g[l], name="in_proj_rwkv")
    a_p = _diff_attn_prompt(qkv_p, lam4, subln_g[l])
    r_p, sh_p, wkv_p = _rwkv7_mix(rw_p.reshape(Bp, Tp, RWKV_COLS), jnp.zeros((Bp, RWKV_COLS), f32),
                                  jnp.zeros((Bp, RWKV_HEADS, RWKV_HD, RWKV_HD), f32), lp)
    h_p = _matmul(jnp.concatenate([a_p, r_p.reshape(Bp * Tp, RWKV_W)], axis=-1), w_out_b, residual=xp, name="out_proj")

    qkv_s = _matmul(xs, w_qkv, gain=norm1_g[l], name="in_proj_attn")
    rw_s = _matmul(xs, w_rwkv, gain=norm1_g[l], name="in_proj_rwkv")
    q_s = qkv_s[:, :DIFF_W].reshape(Bs, Ts, DIFF_MAPS, DIFF_HD)
    k_s = qkv_s[:, DIFF_W:2 * DIFF_W].reshape(Bs, Ts, DIFF_MAPS, DIFF_HD)
    v_s = qkv_s[:, 2 * DIFF_W:].reshape(Bs, Ts, DIFF_HEADS, 2 * DIFF_HD)
    a_s = _diff_attn_sample(q_s, k_s, v_s, cache_k[l], cache_v[l], page_table, lam4, subln_g[l])
    r_s, sh_s, wkv_s = _rwkv7_mix(rw_s.reshape(Bs, Ts, RWKV_COLS), state_shift[l], state_wkv[l], lp)
    h_s = _matmul(jnp.concatenate([a_s.reshape(Bs * Ts, DIFF_W), r_s.reshape(Bs * Ts, RWKV_W)], axis=-1), w_out_b,
                  residual=xs, name="out_proj")

    cq_p = _matmul(h_p, w_cq_b, gain=norm2_g[l], name="cross_q")
    cq_s = _matmul(h_s, w_cq_b, gain=norm2_g[l], name="cross_q")
    co_p = _cross_attn_prompt(cq_p, mk_p, mv_p)
    co_s = _cross_attn_sample(cq_s.reshape(Bs, Ts, D), cache_mem_k[l], cache_mem_v[l]).reshape(Bs * Ts, D)
    h_p = _matmul(co_p, w_co_b, residual=h_p, name="cross_out")
    h_s = _matmul(co_s, w_co_b, residual=h_s, name="cross_out")

    h_all = jnp.concatenate([h_p, h_s], axis=0)
    xn_b, r2, e2, e1, wd = _peer_route(h_all, norm3_g[l], w_pq_b, keys_b)
    y_all = _peer_mix(xn_b, u_b, v_b, r2, e2, e1, wd, h_all, final_g)
    y_prompt = y_all[:Bp * Tp].reshape(Bp, Tp, D)
    y_sample = y_all[Bp * Tp:].reshape(Bs, Ts, D)

    k_p = qkv_p[:, DIFF_W:2 * DIFF_W].reshape(Bp, Tp, DIFF_MAPS, DIFF_HD)
    v_p = qkv_p[:, 2 * DIFF_W:].reshape(Bp, Tp, DIFF_HEADS, 2 * DIFF_HD)
    mem_shape = (Bp, -1, MEM_HEADS, MEM_HD)
    return (y_prompt, y_sample, k_p[None], v_p[None], k_s[None], v_s[None],
            wkv_p[None], sh_p[None], wkv_s[None], sh_s[None], mk_p.reshape(mem_shape)[None], mv_p.reshape(mem_shape)[None])
```
